```python
import math, functools
import jax, jax.numpy as jnp
from jax import lax
import numpy as np

D_MODEL = 2048
BATCH = 1
SEQ = 8192
DEPTH = 2
DEC_BATCH = 32
DEC_SEQ = 1
PAST_LEN = 8192
PAGE_SIZE = 128

N_EVEN = (DEPTH + 1) // 2
N_ODD = DEPTH // 2
D_A = D_MODEL // 2
A_BLOCKS = 8
A_BLOCK_W = D_A // A_BLOCKS
CONV_W = 4
LRU_C = 8.0
H_B = 8
HD_B = (D_MODEL // 2) // H_B
D_B = H_B * HD_B
B_PATTERNS = ((128, 1), (512, 4), (2048, 16))
B_BLOCK = 128
WIN_MAX = 2048
N_BUCKETS = 32
REL_MAX_DIST = 2048
H_C = 4
HD_C = (D_MODEL // 2) // H_C
D_C = H_C * HD_C
C_CHUNK = 128
H_D = 8
HD_D = 128
D_D = H_D * HD_D
D_BLOCK = 128
RMS_EPS = 1e-6
IN_EVEN = 2 * D_A + 4 * D_B
IN_ODD = 5 * D_C + 2 * H_C + 4 * D_D + H_D

kernel_name = 'hybrid_rglru_dilated_mlstm_fox_step'

F32 = jnp.float32


def _rmsnorm(x, g):
    xf = x.astype(F32)
    y = xf * lax.rsqrt(jnp.mean(xf * xf, axis=-1, keepdims=True) + RMS_EPS) * g.astype(F32)
    return y.astype(x.dtype)


def _split(z, sizes):
    idx = np.cumsum(sizes)[:-1].tolist()
    return jnp.split(z, idx, axis=-1)


def _t5_bucket(dist):
    exact = N_BUCKETS // 2
    n = jnp.maximum(dist, 1).astype(F32)
    large = exact + (jnp.log(n / exact) / math.log(REL_MAX_DIST / exact) * (N_BUCKETS - exact)).astype(jnp.int32)
    return jnp.where(dist < exact, dist, jnp.minimum(large, N_BUCKETS - 1))


def _lin_combine(c1, c2):
    a1, b1 = c1
    a2, b2 = c2
    return a1 * a2, a2 * b1 + b2


def _rglru(xa, h0, conv0, conv_w, conv_b, gate_w, gate_b, lam):
    L = xa.shape[1]
    xp = jnp.concatenate([conv0.astype(xa.dtype), xa], axis=1)
    xc = sum(conv_w[t] * xp[:, t:t + L] for t in range(CONV_W)) + conv_b
    conv_new = xp[:, L:]
    xb = xc.reshape(xc.shape[0], L, A_BLOCKS, A_BLOCK_W)
    gates = jnp.einsum('blnc,gncd->gblnd', xb, gate_w).reshape(2, *xc.shape) + gate_b[:, None, None, :]
    r = jax.nn.sigmoid(gates[0].astype(F32))
    i = jax.nn.sigmoid(gates[1].astype(F32))
    log_a = -LRU_C * r * jax.nn.softplus(-lam.astype(F32))
    a = jnp.exp(log_a)
    b = jnp.sqrt(-jnp.expm1(2.0 * log_a)) * i * xc.astype(F32)
    b = b.at[:, 0].add(a[:, 0] * h0.astype(F32))
    _, h = lax.associative_scan(_lin_combine, (a, b), axis=1)
    return h, h[:, -1], conv_new


def _combine_groups(outs, lses):
    w = jax.nn.softmax(jnp.stack(lses), axis=0)
    return jnp.sum(w[..., None] * jnp.stack(outs), axis=0)


def _dilated_prompt(q, k, v, rel_bias):
    Bn, S, H, E = q.shape
    scale = E ** -0.5
    qi = jnp.arange(B_BLOCK)[:, None]
    ki = jnp.arange(2 * B_BLOCK)[None, :]
    rel = qi + B_BLOCK - ki
    outs, lses = [], []
    for win, dil in B_PATTERNS:
        unit = dil * B_BLOCK
        Sp = -(-S // unit) * unit
        nb = Sp // unit

        def to_blocks(t):
            t = jnp.pad(t, ((0, 0), (0, Sp - S), (0, 0), (0, 0)))
            t = t.reshape(Bn, Sp // dil, dil, H, E).transpose(0, 2, 1, 3, 4)
            return t.reshape(Bn, dil, nb, B_BLOCK, H, E)

        def with_prev(t):
            prev = jnp.pad(t[:, :, :-1], ((0, 0), (0, 0), (1, 0), (0, 0), (0, 0), (0, 0)))
            return jnp.concatenate([prev, t], axis=3)

        def from_blocks(t):
            t = t.reshape(Bn, dil, Sp // dil, *t.shape[4:])
            t = jnp.swapaxes(t, 1, 2).reshape(Bn, Sp, *t.shape[3:])
            return t[:, :S]

        qb = to_blocks(q).astype(F32)
        kw = with_prev(to_blocks(k)).astype(F32)
        vw = with_prev(to_blocks(v)).astype(F32)
        valid = (rel >= 0) & (rel <= win // dil)
        bias = jnp.moveaxis(rel_bias[_t5_bucket(jnp.maximum(rel, 0) * dil)], -1, 0).astype(F32)
        first = (jnp.arange(nb)[:, None] > 0) | (ki >= B_BLOCK)
        mask = valid[None, None] & first[:, None, None, :]
        s = jnp.einsum('bcnqhe,bcnkhe->bcnhqk', qb, kw) * scale + bias
        s = jnp.where(mask, s, -jnp.inf)
        m = jnp.max(s, axis=-1, keepdims=True)
        p = jnp.exp(s - m)
        den = jnp.sum(p, axis=-1)
        o = jnp.einsum('bcnhqk,bcnkhe->bcnqhe', p, vw) / jnp.moveaxis(den, -2, -1)[..., None]
        lse = jnp.moveaxis(m[..., 0] + jnp.log(den), -2, -1)
        outs.append(from_blocks(o))
        lses.append(from_blocks(lse))
    return _combine_groups(outs, lses)


def _dilated_sample(q, k_all, v_all, rel_bias):
    Bn, DS, H, E = q.shape
    WB = k_all.shape[1] - DS
    scale = E ** -0.5
    outs, lses = [], []
    for win, dil in B_PATTERNS:
        m = jnp.arange(win // dil + 1)
        idx = (WB + jnp.arange(DS))[:, None] - m[None, :] * dil
        valid = idx >= 0
        idc = jnp.maximum(idx, 0)
        kg = k_all[:, idc].astype(F32)
        vg = v_all[:, idc].astype(F32)
        bias = rel_bias[_t5_bucket(m * dil)].astype(F32)
        s = jnp.einsum('buhe,bumhe->buhm', q.astype(F32), kg) * scale + bias.T[None, None]
        s = jnp.where(valid[None, :, None, :], s, -jnp.inf)
        lse = jax.nn.logsumexp(s, axis=-1)
        p = jnp.exp(s - lse[..., None])
        outs.append(jnp.einsum('buhm,bumhe->buhe', p, vg))
        lses.append(lse)
    return _combine_groups(outs, lses)


def _attend_b_prompt(q, k, v, rel_bias):
    wbp = min(WIN_MAX, k.shape[1])
    return _dilated_prompt(q, k, v, rel_bias), k[:, -wbp:], v[:, -wbp:]


def _attend_b_sample(q, k, v, k_buf, v_buf, rel_bias):
    wb = k_buf.shape[1]
    k_all = jnp.concatenate([k_buf, k], axis=1)
    v_all = jnp.concatenate([v_buf, v], axis=1)
    return _dilated_sample(q, k_all, v_all, rel_bias), k_all[:, -wb:], v_all[:, -wb:]


def _mlstm(q, k, v, i_pre, logf, C0, n0, m0):
    Bn, L, H, DK = q.shape
    Lc = C_CHUNK if L % C_CHUNK == 0 else L
    nc = L // Lc
    q = q.astype(F32)
    k = k.astype(F32) * (DK ** -0.5)
    v = v.astype(F32)
    causal = jnp.tril(jnp.ones((Lc, Lc), dtype=bool))[None, :, :, None]

    def chunks(t):
        return jnp.moveaxis(t.reshape(Bn, nc, Lc, *t.shape[2:]), 1, 0)

    def step(carry, inp):
        C, n, m = carry
        qc, kc, vc, ic, fc = inp
        b = jnp.cumsum(fc, axis=1)
        Dm = b[:, :, None, :] - b[:, None, :, :] + ic[:, None, :, :]
        Dm = jnp.where(causal, Dm, -jnp.inf)
        inter = b + m[:, None, :]
        mt = jnp.maximum(inter, jnp.max(Dm, axis=2))
        Wqk = jnp.exp(Dm - mt[:, :, None, :]) * jnp.einsum('bthe,bshe->btsh', qc, kc)
        g = jnp.exp(inter - mt)
        num = jnp.einsum('btsh,bshv->bthv', Wqk, vc) + g[..., None] * jnp.einsum('bhve,bthe->bthv', C, qc)
        den = jnp.sum(Wqk, axis=2) + g * jnp.einsum('bhe,bthe->bth', n, qc)
        h = num / jnp.maximum(jnp.abs(den), jnp.exp(-mt))[..., None]
        bT = b[:, -1]
        wlast = bT[:, None, :] - b + ic
        m_new = jnp.maximum(bT + m, jnp.max(wlast, axis=1))
        ws = jnp.exp(wlast - m_new[:, None, :])
        gT = jnp.exp(bT + m - m_new)
        C_new = gT[..., None, None] * C + jnp.einsum('bsh,bshv,bshe->bhve', ws, vc, kc)
        n_new = gT[..., None] * n + jnp.einsum('bsh,bshe->bhe', ws, kc)
        return (C_new, n_new, m_new), h

    xs = (chunks(q), chunks(k), chunks(v), chunks(i_pre.astype(F32)), chunks(logf.astype(F32)))
    (C, n, m), hs = lax.scan(step, (C0.astype(F32), n0.astype(F32), m0.astype(F32)), xs)
    h = jnp.moveaxis(hs, 0, 1).reshape(Bn, L, H, v.shape[-1])
    return h, C, n, m


def _fox_prompt(q, k, v, logf):
    Bn, S, H, E = q.shape
    scale = E ** -0.5
    Fk = jnp.swapaxes(jnp.cumsum(logf.astype(F32), axis=1), 1, 2)
    nb = S // D_BLOCK
    qb = jnp.moveaxis(q.reshape(Bn, nb, D_BLOCK, H, E), 1, 0)
    Fq = jnp.moveaxis(Fk.reshape(Bn, H, nb, D_BLOCK), 2, 0)
    kf, vf = k.astype(F32), v.astype(F32)
    kpos = jnp.arange(S)

    def block(args):
        qi, fi, bi = args
        s = jnp.einsum('bqhe,bkhe->bhqk', qi.astype(F32), kf) * scale + fi[..., None] - Fk[:, :, None, :]
        qpos = bi * D_BLOCK + jnp.arange(D_BLOCK)
        s = jnp.where(kpos[None, :] <= qpos[:, None], s, -jnp.inf)
        p = jax.nn.softmax(s, axis=-1)
        return jnp.einsum('bhqk,bkhe->bqhe', p, vf)

    o = lax.map(block, (qb, Fq, jnp.arange(nb)))
    return jnp.moveaxis(o, 0, 1).reshape(Bn, S, H, E)


def _gather_pages(pool, page_table):
    g = pool[page_table]
    return g.reshape(g.shape[0], g.shape[1] * g.shape[2], *g.shape[3:])


def _attend_d_sample(q, k, v, logf, k_pool, v_pool, logf_pool, page_table):
    Bn, DS, H, E = q.shape
    k_all = jnp.concatenate([_gather_pages(k_pool, page_table), k], axis=1).astype(F32)
    v_all = jnp.concatenate([_gather_pages(v_pool, page_table), v], axis=1).astype(F32)
    lf_all = jnp.concatenate([_gather_pages(logf_pool, page_table).astype(F32), logf.astype(F32)], axis=1)
    T = k_all.shape[1]
    P = T - DS
    F = jnp.swapaxes(jnp.cumsum(lf_all, axis=1), 1, 2)
    s = jnp.einsum('bqhe,bkhe->bhqk', q.astype(F32), k_all) * (E ** -0.5) + F[:, :, P:, None] - F[:, :, None, :]
    mask = jnp.arange(T)[None, :] <= (P + jnp.arange(DS))[:, None]
    s = jnp.where(mask, s, -jnp.inf)
    p = jax.nn.softmax(s, axis=-1)
    return jnp.einsum('bhqk,bkhe->bqhe', p, v_all)


def _even_layer(h, h0, conv0, attend, w_in, w_out, conv_w, conv_b, gate_w, gate_b, lam):
    Bn, L, _ = h.shape
    xa, ga, q, k, v, gb = _split(h @ w_in, [D_A, D_A, D_B, D_B, D_B, D_B])
    ya, h_last, conv_new = _rglru(xa, h0, conv0, conv_w, conv_b, gate_w, gate_b, lam)
    heads = lambda t: t.reshape(Bn, L, H_B, HD_B)
    yb, k_state, v_state = attend(heads(q), heads(k), heads(v))
    y = jnp.concatenate([ya * jax.nn.silu(ga.astype(F32)),
                         yb.reshape(Bn, L, D_B) * jax.nn.silu(gb.astype(F32))], axis=-1)
    return y @ w_out, h_last, conv_new, k_state, v_state


def _odd_layer(h, C0, n0, m0, attend, w_in, w_out, c_gate_b, d_f_b):
    Bn, L, _ = h.shape
    cq, ck, cv, co, cg, ci, cf, dq, dk, dv, dg, df = _split(
        h @ w_in, [D_C, D_C, D_C, D_C, D_C, H_C, H_C, D_D, D_D, D_D, D_D, H_D])
    hc_ = lambda t: t.reshape(Bn, L, H_C, HD_C)
    hd_ = lambda t: t.reshape(Bn, L, H_D, HD_D)
    i_pre = ci.astype(F32) + c_gate_b[0]
    logf_c = jax.nn.log_sigmoid(cf.astype(F32) + c_gate_b[1])
    hc, C, n, m = _mlstm(hc_(cq), hc_(ck), hc_(cv), i_pre, logf_c, C0, n0, m0)
    yc = jax.nn.sigmoid(co.astype(F32)) * hc.reshape(Bn, L, D_C)
    logf_d = jax.nn.log_sigmoid(df.astype(F32) + d_f_b)
    k_d, v_d = hd_(dk), hd_(dv)
    yd = attend(hd_(dq), k_d, v_d, logf_d).reshape(Bn, L, D_D)
    y = jnp.concatenate([yc * jax.nn.silu(cg.astype(F32)), yd * jax.nn.silu(dg.astype(F32))], axis=-1)
    return y @ w_out, C, n, m, k_d, v_d, logf_d


def setup_inputs(seed: int = 0) -> dict:
    key = jax.random.key(seed)
    ks = iter(jax.random.split(key, 32))
    nrm = lambda shape, scale: scale * jax.random.normal(next(ks), shape, F32)
    n_pages = PAST_LEN // PAGE_SIZE
    n_used = DEC_BATCH * n_pages
    n_pool = n_used + n_used // 4
    wb = min(WIN_MAX, PAST_LEN)
    x_prompt = nrm((BATCH, SEQ, D_MODEL), 1.0)
    x_sample = nrm((DEC_BATCH, DEC_SEQ, D_MODEL), 1.0)
    state_a_h = nrm((N_EVEN, DEC_BATCH, D_A), 0.5)
    state_a_conv = nrm((N_EVEN, DEC_BATCH, CONV_W - 1, D_A), 1.0)
    cache_b_k = nrm((N_EVEN, DEC_BATCH, wb, H_B, HD_B), 1.0)
    cache_b_v = nrm((N_EVEN, DEC_BATCH, wb, H_B, HD_B), 1.0)
    state_c_C = nrm((N_ODD, DEC_BATCH, H_C, HD_C, HD_C), 0.05)
    state_c_n = nrm((N_ODD, DEC_BATCH, H_C, HD_C), 0.1)
    state_c_m = nrm((N_ODD, DEC_BATCH, H_C), 1.0)
    cache_d_k = nrm((N_ODD, n_pool, PAGE_SIZE, H_D, HD_D), 1.0)
    cache_d_v = nrm((N_ODD, n_pool, PAGE_SIZE, H_D, HD_D), 1.0)
    cache_d_logf = jax.nn.log_sigmoid(3.5 + nrm((N_ODD, n_pool, PAGE_SIZE, H_D), 1.0))
    perm = jax.random.permutation(next(ks), n_pool)
    page_table = perm[:n_used].reshape(DEC_BATCH, n_pages).astype(jnp.int32)
    norm_pre = 1.0 + nrm((DEPTH, D_MODEL), 0.05)
    norm_post = 1.0 + nrm((DEPTH, D_MODEL), 0.05)
    w_in_even = nrm((N_EVEN, D_MODEL, IN_EVEN), D_MODEL ** -0.5)
    w_out_even = nrm((N_EVEN, D_A + D_B, D_MODEL), (D_A + D_B) ** -0.5)
    a_conv_w = nrm((N_EVEN, CONV_W, D_A), CONV_W ** -0.5)
    a_conv_b = nrm((N_EVEN, D_A), 0.02)
    a_gate_w = nrm((N_EVEN, 2, A_BLOCKS, A_BLOCK_W, A_BLOCK_W), A_BLOCK_W ** -0.5)
    a_gate_b = nrm((N_EVEN, 2, D_A), 0.02)
    a0 = jax.random.uniform(next(ks), (N_EVEN, D_A), F32, minval=0.9, maxval=0.999)
    p = a0 ** (1.0 / LRU_C)
    a_lambda = jnp.log(p) - jnp.log1p(-p)
    rel_bias = nrm((N_BUCKETS, H_B), 0.5)
    w_in_odd = nrm((N_ODD, D_MODEL, IN_ODD), D_MODEL ** -0.5)
    w_out_odd = nrm((N_ODD, D_C + D_D, D_MODEL), (D_C + D_D) ** -0.5)
    c_gate_b = jnp.stack([nrm((N_ODD, H_C), 0.1),
                          jnp.linspace(3.0, 6.0, H_C, dtype=F32)[None] + nrm((N_ODD, H_C), 0.1)], axis=1)
    d_f_b = jnp.linspace(2.0, 5.0, H_D, dtype=F32)[None] + nrm((N_ODD, H_D), 0.1)
    return {'x_prompt': x_prompt, 'x_sample': x_sample,
            'state_a_h': state_a_h, 'state_a_conv': state_a_conv,
            'cache_b_k': cache_b_k, 'cache_b_v': cache_b_v,
            'state_c_C': state_c_C, 'state_c_n': state_c_n, 'state_c_m': state_c_m,
            'cache_d_k': cache_d_k, 'cache_d_v': cache_d_v, 'cache_d_logf': cache_d_logf,
            'page_table': page_table,
            'norm_pre': norm_pre, 'norm_post': norm_post,
            'w_in_even': w_in_even, 'w_out_even': w_out_even,
            'a_conv_w': a_conv_w, 'a_conv_b': a_conv_b, 'a_gate_w': a_gate_w, 'a_gate_b': a_gate_b,
            'a_lambda': a_lambda, 'rel_bias': rel_bias,
            'w_in_odd': w_in_odd, 'w_out_odd': w_out_odd, 'c_gate_b': c_gate_b, 'd_f_b': d_f_b}


def reference(x_prompt, x_sample, state_a_h, state_a_conv, cache_b_k, cache_b_v, state_c_C, state_c_n,
              state_c_m, cache_d_k, cache_d_v, cache_d_logf, page_table, norm_pre, norm_post, w_in_even,
              w_out_even, a_conv_w, a_conv_b, a_gate_w, a_gate_b, a_lambda, rel_bias, w_in_odd, w_out_odd,
              c_gate_b, d_f_b):
    xp, xs = x_prompt, x_sample
    bp = xp.shape[0]
    ah_p, ah_s, ac_p, ac_s, bk_p, bv_p, bk_s, bv_s = [], [], [], [], [], [], [], []
    cC_p, cn_p, cm_p, cC_s, cn_s, cm_s = [], [], [], [], [], []
    dk_p, dv_p, dl_p, dk_s, dv_s, dl_s = [], [], [], [], [], []
    for l in range(DEPTH):
        j = l // 2
        hn_p = _rmsnorm(xp, norm_pre[l])
        hn_s = _rmsnorm(xs, norm_pre[l])
        if l % 2 == 0:
            wts = (w_in_even[j], w_out_even[j], a_conv_w[j], a_conv_b[j], a_gate_w[j], a_gate_b[j], a_lambda[j])
            yp, h1, c1, k1, v1 = _even_layer(
                hn_p, jnp.zeros((bp, D_A), F32), jnp.zeros((bp, CONV_W - 1, D_A), xp.dtype),
                functools.partial(_attend_b_prompt, rel_bias=rel_bias), *wts)
            ys, h2, c2, k2, v2 = _even_layer(
                hn_s, state_a_h[j], state_a_conv[j],
                functools.partial(_attend_b_sample, k_buf=cache_b_k[j], v_buf=cache_b_v[j], rel_bias=rel_bias), *wts)
            ah_p.append(h1); ac_p.append(c1); bk_p.append(k1); bv_p.append(v1)
            ah_s.append(h2); ac_s.append(c2); bk_s.append(k2); bv_s.append(v2)
        else:
            wts = (w_in_odd[j], w_out_odd[j], c_gate_b[j], d_f_b[j])
            yp, C1, n1, m1, k1, v1, f1 = _odd_layer(
                hn_p, jnp.zeros((bp, H_C, HD_C, HD_C), F32), jnp.zeros((bp, H_C, HD_C), F32),
                jnp.zeros((bp, H_C), F32), _fox_prompt, *wts)
            ys, C2, n2, m2, k2, v2, f2 = _odd_layer(
                hn_s, state_c_C[j], state_c_n[j], state_c_m[j],
                functools.partial(_attend_d_sample, k_pool=cache_d_k[j], v_pool=cache_d_v[j],
                                  logf_pool=cache_d_logf[j], page_table=page_table), *wts)
            cC_p.append(C1); cn_p.append(n1); cm_p.append(m1); dk_p.append(k1); dv_p.append(v1); dl_p.append(f1)
            cC_s.append(C2); cn_s.append(n2); cm_s.append(m2); dk_s.append(k2); dv_s.append(v2); dl_s.append(f2)
        xp = xp + _rmsnorm(yp, norm_post[l]).astype(xp.dtype)
        xs = xs + _rmsnorm(ys, norm_post[l]).astype(xs.dtype)
    return (xp, xs,
            jnp.stack(ah_p), jnp.stack(ah_s), jnp.stack(ac_p), jnp.stack(ac_s),
            jnp.stack(bk_p), jnp.stack(bv_p), jnp.stack(bk_s), jnp.stack(bv_s),
            jnp.stack(cC_p), jnp.stack(cn_p), jnp.stack(cm_p), jnp.stack(cC_s), jnp.stack(cn_s), jnp.stack(cm_s),
            jnp.stack(dk_p), jnp.stack(dv_p), jnp.stack(dl_p), jnp.stack(dk_s), jnp.stack(dv_s), jnp.stack(dl_s))
```

```python
import functools
import math

import numpy as np
import jax
import jax.numpy as jnp
from jax import lax
from jax.experimental import pallas as pl
from jax.experimental.pallas import tpu as pltpu

F32 = jnp.float32
BF16 = jnp.bfloat16

LANES = 128
VMEM_LIMIT_BYTES = 56 * 2**20

RMS_EPS = 1e-6
LRU_C = 8.0
A_BLOCKS = 8
CONV_W = 4
H_B = 8
B_BLOCK = 128
B_DILATIONS = (1, 4, 16)
N_BUCKETS = 32
REL_MAX_DIST = 2048
H_C = 4
C_CHUNK = 128
H_D = 8
PAGE_SIZE = 128
NEG_INF = float("-inf")


def _params(*semantics):
    return pltpu.CompilerParams(dimension_semantics=semantics, vmem_limit_bytes=VMEM_LIMIT_BYTES)


def _sigmoid(x):
    return 1.0 / (1.0 + jnp.exp(-x))


def _silu(x):
    return x * _sigmoid(x)


def _softplus(x):
    return jnp.maximum(x, 0.0) + jnp.log1p(jnp.exp(-jnp.abs(x)))


def _log_sigmoid(x):
    return -_softplus(-x)


def _dot(a, b):
    return jnp.dot(a, b, preferred_element_type=F32)


def _dot_nt(a, b):
    return lax.dot_general(a, b, (((1,), (1,)), ((), ())), preferred_element_type=F32)


def _dot_tn(a, b):
    return lax.dot_general(a, b, (((0,), (0,)), ((), ())), preferred_element_type=F32)


def _norm_proj_kernel(x_ref, g_ref, w_ref, *rest, with_gates):
    if with_gates:
        wg_ref, o_ref, og_ref, hn_ref = rest
    else:
        o_ref, hn_ref = rest

    @pl.when(pl.program_id(1) == 0)
    def _():
        x = x_ref[...]
        ms = jnp.mean(x * x, axis=-1, keepdims=True)
        hn = (x * lax.rsqrt(ms + RMS_EPS) * g_ref[...]).astype(BF16)
        hn_ref[...] = hn
        if with_gates:
            og_ref[...] = _dot(hn, wg_ref[...])

    o_ref[...] = _dot(hn_ref[...], w_ref[...])


def _norm_proj(x, g, w, wg=None, *, tm, tn):
    m, d = x.shape
    n = w.shape[1]
    with_gates = wg is not None
    in_specs = [pl.BlockSpec((tm, d), lambda i, j: (i, 0)),
                pl.BlockSpec((1, d), lambda i, j: (0, 0)),
                pl.BlockSpec((d, tn), lambda i, j: (0, j))]
    out_specs = [pl.BlockSpec((tm, tn), lambda i, j: (i, j))]
    out_shape = [jax.ShapeDtypeStruct((m, n), F32)]
    args = [x, g, w]
    if with_gates:
        in_specs.append(pl.BlockSpec((d, LANES), lambda i, j: (0, 0)))
        out_specs.append(pl.BlockSpec((tm, LANES), lambda i, j: (i, 0)))
        out_shape.append(jax.ShapeDtypeStruct((m, LANES), F32))
        args.append(wg)
    outs = pl.pallas_call(
        functools.partial(_norm_proj_kernel, with_gates=with_gates),
        grid=(m // tm, n // tn),
        in_specs=in_specs, out_specs=out_specs, out_shape=out_shape,
        scratch_shapes=[pltpu.VMEM((tm, d), BF16)],
        compiler_params=_params("parallel", "arbitrary"),
        name="norm_proj",
    )(*args)
    return outs if with_gates else outs[0]


def _out_proj_kernel(ya_ref, yb_ref, wa_ref, wb_ref, g_ref, x_ref, o_ref):
    y = _dot(ya_ref[...], wa_ref[...]) + _dot(yb_ref[...], wb_ref[...])
    ms = jnp.mean(y * y, axis=-1, keepdims=True)
    o_ref[...] = x_ref[...] + y * lax.rsqrt(ms + RMS_EPS) * g_ref[...]


def _out_proj(ya, yb, wa, wb, g, x, *, tm):
    m, d = x.shape
    ka, kb = ya.shape[1], yb.shape[1]
    return pl.pallas_call(
        _out_proj_kernel,
        grid=(m // tm,),
        in_specs=[pl.BlockSpec((tm, ka), lambda i: (i, 0)),
                  pl.BlockSpec((tm, kb), lambda i: (i, 0)),
                  pl.BlockSpec((ka, d), lambda i: (0, 0)),
                  pl.BlockSpec((kb, d), lambda i: (0, 0)),
                  pl.BlockSpec((1, d), lambda i: (0, 0)),
                  pl.BlockSpec((tm, d), lambda i: (i, 0))],
        out_specs=pl.BlockSpec((tm, d), lambda i: (i, 0)),
        out_shape=jax.ShapeDtypeStruct((m, d), F32),
        compiler_params=_params("parallel"),
        name="out_proj",
    )(ya, yb, wa, wb, g, x)


def _rglru_gates(xc, ga_w_ref, gate_b_ref, sp_lam):
    d_a = xc.shape[1]
    bw = d_a // A_BLOCKS
    pre_r, pre_i = [], []
    for n in range(A_BLOCKS):
        pre = _dot(xc[:, n * bw:(n + 1) * bw].astype(BF16), ga_w_ref[n])
        pre_r.append(pre[:, :bw])
        pre_i.append(pre[:, bw:])
    r = _sigmoid(jnp.concatenate(pre_r, axis=1) + gate_b_ref[0:1, :])
    i = _sigmoid(jnp.concatenate(pre_i, axis=1) + gate_b_ref[1:2, :])
    log_a = -LRU_C * r * sp_lam
    a = jnp.exp(log_a)
    b = jnp.sqrt(-jnp.tanh(log_a) * (a * a + 1.0)) * i * xc
    return a, b


def _rglru_prompt_kernel(xa_ref, ga_ref, cw_ref, cb_ref, gw_ref, gb_ref, lam_ref,
                         ya_ref, hlast_ref, conv_ref, h_s, xprev_s):
    step = pl.program_id(0)
    tb = xa_ref.shape[0]

    @pl.when(step == 0)
    def _():
        h_s[...] = jnp.zeros_like(h_s)
        xprev_s[...] = jnp.zeros_like(xprev_s)

    xa = xa_ref[...]
    xcat = jnp.concatenate([xprev_s[...], xa], axis=0)
    xc = (cw_ref[0:1, :] * xcat[5:5 + tb] + cw_ref[1:2, :] * xcat[6:6 + tb]
          + cw_ref[2:3, :] * xcat[7:7 + tb] + cw_ref[3:4, :] * xa) + cb_ref[...]
    a, b = _rglru_gates(xc, gw_ref, gb_ref, _softplus(-lam_ref[...]))

    row = lax.broadcasted_iota(jnp.int32, a.shape, 0)
    s = 1
    while s < tb:
        keep = row >= s
        a_sh = jnp.where(keep, pltpu.roll(a, s, axis=0), 1.0)
        b_sh = jnp.where(keep, pltpu.roll(b, s, axis=0), 0.0)
        b = a * b_sh + b
        a = a * a_sh
        s *= 2
    h = a * h_s[...] + b
    ya_ref[...] = (h * _silu(ga_ref[...])).astype(ya_ref.dtype)
    h_s[...] = h[tb - 1:tb, :]
    hlast_ref[...] = h[tb - 1:tb, :]
    conv_ref[...] = xa[tb - (CONV_W - 1):tb, :]
    xprev_s[...] = xa[tb - 8:tb, :]


def _rglru_prompt(z, conv_w, conv_b, gw, gate_b, lam, *, d_a, tb):
    s = z.shape[0]
    bw = d_a // A_BLOCKS
    full = lambda shape: pl.BlockSpec(shape, lambda i: (0,) * len(shape))
    return pl.pallas_call(
        _rglru_prompt_kernel,
        grid=(s // tb,),
        in_specs=[pl.BlockSpec((tb, d_a), lambda i: (i, 0)),
                  pl.BlockSpec((tb, d_a), lambda i: (i, 1)),
                  full((CONV_W, d_a)), full((1, d_a)), full((A_BLOCKS, bw, 2 * bw)),
                  full((2, d_a)), full((1, d_a))],
        out_specs=[pl.BlockSpec((tb, d_a), lambda i: (i, 0)),
                   full((1, d_a)), full((CONV_W - 1, d_a))],
        out_shape=[jax.ShapeDtypeStruct((s, d_a), BF16),
                   jax.ShapeDtypeStruct((1, d_a), F32),
                   jax.ShapeDtypeStruct((CONV_W - 1, d_a), F32)],
        scratch_shapes=[pltpu.VMEM((1, d_a), F32), pltpu.VMEM((8, d_a), F32)],
        compiler_params=_params("arbitrary"),
        name="rglru_prompt",
    )(z, z, conv_w, conv_b, gw, gate_b, lam)


def _rglru_sample_kernel(xa_ref, ga_ref, c0_ref, h0_ref, cw_ref, cb_ref, gw_ref, gb_ref, lam_ref,
                         ya_ref, h_ref, conv_ref):
    xa = xa_ref[...]
    xc = (cw_ref[0:1, :] * c0_ref[0] + cw_ref[1:2, :] * c0_ref[1]
          + cw_ref[2:3, :] * c0_ref[2] + cw_ref[3:4, :] * xa) + cb_ref[...]
    a, b = _rglru_gates(xc, gw_ref, gb_ref, _softplus(-lam_ref[...]))
    h = a * h0_ref[...] + b
    ya_ref[...] = (h * _silu(ga_ref[...])).astype(ya_ref.dtype)
    h_ref[...] = h
    conv_ref[0] = c0_ref[1]
    conv_ref[1] = c0_ref[2]
    conv_ref[2] = xa


def _rglru_sample(z, conv0_t, h0, conv_w, conv_b, gw, gate_b, lam, *, d_a):
    bsz = z.shape[0]
    bw = d_a // A_BLOCKS
    full = lambda shape: pl.BlockSpec(shape, lambda i: (0,) * len(shape))
    return pl.pallas_call(
        _rglru_sample_kernel,
        grid=(1,),
        in_specs=[pl.BlockSpec((bsz, d_a), lambda i: (0, 0)),
                  pl.BlockSpec((bsz, d_a), lambda i: (0, 1)),
                  full((CONV_W - 1, bsz, d_a)), full((bsz, d_a)),
                  full((CONV_W, d_a)), full((1, d_a)), full((A_BLOCKS, bw, 2 * bw)),
                  full((2, d_a)), full((1, d_a))],
        out_specs=[full((bsz, d_a)), full((bsz, d_a)), full((CONV_W - 1, bsz, d_a))],
        out_shape=[jax.ShapeDtypeStruct((bsz, d_a), BF16),
                   jax.ShapeDtypeStruct((bsz, d_a), F32),
                   jax.ShapeDtypeStruct((CONV_W - 1, bsz, d_a), F32)],
        compiler_params=_params("arbitrary"),
        name="rglru_sample",
    )(z, z, conv0_t, h0, conv_w, conv_b, gw, gate_b, lam)


def _t5_bucket(dist):
    exact = N_BUCKETS // 2
    n = np.maximum(dist, 1).astype(np.float32)
    large = exact + (np.log(n / np.float32(exact)) / np.float32(math.log(REL_MAX_DIST / exact))
                     * np.float32(N_BUCKETS - exact)).astype(np.int32)
    return np.where(dist < exact, dist, np.minimum(large, N_BUCKETS - 1))


def _dil_prompt_bias(rel_bias):
    qi = np.arange(B_BLOCK)[:, None]
    ki = np.arange(2 * B_BLOCK)[None, :]
    rel = qi + B_BLOCK - ki
    valid = (rel >= 0) & (rel <= B_BLOCK)
    tabs = []
    for dil in B_DILATIONS:
        idx = _t5_bucket(np.maximum(rel, 0) * dil)
        b = jnp.moveaxis(rel_bias[idx], -1, 0)
        tabs.append(jnp.where(valid[None], b, NEG_INF))
    return jnp.stack(tabs)


def _dil_prompt_kernel(q_ref, kc_ref, kp_ref, vc_ref, vp_ref, gb_ref, bias_ref, o_ref,
                       m_s, l_s, acc_s):
    tile = q_ref.shape[0]
    hd = q_ref.shape[1]
    scale = hd ** -0.5
    not_first = pl.program_id(0) > 0

    def rows(start, d):
        return pl.ds(start, B_BLOCK, stride=d) if d > 1 else pl.ds(start, B_BLOCK)

    def attend(p_idx, d, q_start, prev_k, prev_v, prev_start, cur_start, prev_is_other_tile):
        r = rows(q_start, d)
        q = (q_ref[r, :] * scale).astype(BF16)
        kp = prev_k[rows(prev_start, d), :].astype(BF16)
        vp = prev_v[rows(prev_start, d), :].astype(BF16)
        kc = kc_ref[rows(cur_start, d), :].astype(BF16)
        vc = vc_ref[rows(cur_start, d), :].astype(BF16)
        bias = bias_ref[p_idx, 0]
        s_p = _dot_nt(q, kp) + bias[:, :B_BLOCK]
        s_c = _dot_nt(q, kc) + bias[:, B_BLOCK:]
        if prev_is_other_tile:
            s_p = jnp.where(not_first, s_p, NEG_INF)
        m_b = jnp.maximum(jnp.max(s_p, axis=1, keepdims=True), jnp.max(s_c, axis=1, keepdims=True))
        p_p = jnp.exp(s_p - m_b)
        p_c = jnp.exp(s_c - m_b)
        den = jnp.sum(p_p, axis=1, keepdims=True) + jnp.sum(p_c, axis=1, keepdims=True)
        o = _dot(p_p.astype(BF16), vp) + _dot(p_c.astype(BF16), vc)
        m_b = jnp.broadcast_to(m_b, (B_BLOCK, hd))
        den = jnp.broadcast_to(den, (B_BLOCK, hd))
        if p_idx == 0:
            m_s[r, :] = m_b
            l_s[r, :] = den
            acc_s[r, :] = o
        else:
            m_o = m_s[r, :]
            m_n = jnp.maximum(m_o, m_b)
            al = jnp.exp(m_o - m_n)
            be = jnp.exp(m_b - m_n)
            m_s[r, :] = m_n
            l_s[r, :] = l_s[r, :] * al + den * be
            acc_s[r, :] = acc_s[r, :] * al + o * be

    for p_idx, d in enumerate(B_DILATIONS):
        unit = d * B_BLOCK
        n_units = tile // unit
        if d == 1:
            attend(p_idx, 1, 0, kp_ref, vp_ref, tile - B_BLOCK, 0, True)

            def body(j, carry):
                cur = pl.multiple_of(j * B_BLOCK, B_BLOCK)
                prev = pl.multiple_of((j - 1) * B_BLOCK, B_BLOCK)
                attend(0, 1, cur, kc_ref, vc_ref, prev, cur, False)
                return carry

            lax.fori_loop(1, n_units, body, 0)
        else:
            for u in range(n_units):
                for c in range(d):
                    if u == 0:
                        attend(p_idx, d, c, kp_ref, vp_ref, tile - unit + c, c, True)
                    else:
                        attend(p_idx, d, u * unit + c, kc_ref, vc_ref, (u - 1) * unit + c, u * unit + c, False)

    o_ref[...] = (acc_s[...] / l_s[...] * _silu(gb_ref[...])).astype(o_ref.dtype)


def _dil_prompt(z, bias, *, col0, tile):
    s = z.shape[0]
    hd = LANES
    qc, kc, vc, gc = (col0 + i * H_B for i in range(4))
    prev = lambda t: jnp.maximum(t - 1, 0)
    blk = lambda shape, fn: pl.BlockSpec(shape, fn)
    return pl.pallas_call(
        _dil_prompt_kernel,
        grid=(s // tile, H_B),
        in_specs=[blk((tile, hd), lambda t, h: (t, qc + h)),
                  blk((tile, hd), lambda t, h: (t, kc + h)),
                  blk((tile, hd), lambda t, h: (prev(t), kc + h)),
                  blk((tile, hd), lambda t, h: (t, vc + h)),
                  blk((tile, hd), lambda t, h: (prev(t), vc + h)),
                  blk((tile, hd), lambda t, h: (t, gc + h)),
                  blk((len(B_DILATIONS), 1, B_BLOCK, 2 * B_BLOCK), lambda t, h: (0, h, 0, 0))],
        out_specs=blk((tile, hd), lambda t, h: (t, h)),
        out_shape=jax.ShapeDtypeStruct((s, H_B * hd), BF16),
        scratch_shapes=[pltpu.VMEM((tile, hd), F32)] * 3,
        compiler_params=_params("parallel", "parallel"),
        name="dilated_prompt",
    )(z, z, z, z, z, z, bias)


def _dil_sample_kernel(q_ref, kn_ref, vn_ref, gb_ref, *rest):
    n_pat = len(B_DILATIONS)
    k_refs = rest[:n_pat]
    v_refs = rest[n_pat:2 * n_pat]
    bias_ref, bias0_ref, o_ref = rest[2 * n_pat:]
    d_b = q_ref.shape[-1]
    hd = d_b // H_B
    head_of_lane = lax.broadcasted_iota(jnp.int32, (H_B, d_b), 1) // hd
    own = head_of_lane == lax.broadcasted_iota(jnp.int32, (H_B, d_b), 0)
    qbd = jnp.where(own, q_ref[0] * hd ** -0.5, 0.0)
    s_new = jnp.sum(qbd * kn_ref[0], axis=1, keepdims=True) + bias0_ref[...]
    qb = qbd.astype(BF16)
    s = [_dot_nt(qb, k_refs[g][0].astype(BF16)) + bias_ref[g] for g in range(n_pat)]
    m = s_new
    for sg in s:
        m = jnp.maximum(m, jnp.max(sg, axis=1, keepdims=True))
    p_new = n_pat * jnp.exp(s_new - m)
    den = p_new
    num = p_new * vn_ref[0]
    for g in range(n_pat):
        p = jnp.exp(s[g] - m)
        den = den + jnp.sum(p, axis=1, keepdims=True)
        num = num + _dot(p.astype(BF16), v_refs[g][0].astype(BF16))
    out = jnp.sum(jnp.where(own, num / den, 0.0), axis=0, keepdims=True)
    o_ref[0] = (out * _silu(gb_ref[0])).astype(o_ref.dtype)


def _dil_sample_bias(rel_bias):
    j = np.arange(B_BLOCK)
    tabs = [rel_bias[_t5_bucket((B_BLOCK - j) * dil)].T for dil in B_DILATIONS]
    return jnp.stack(tabs), rel_bias[_t5_bucket(np.zeros((1,), np.int64))].T


def _dil_sample(z3, k_buf, v_buf, bias, bias0, *, col0):
    bsz, wb, d_b = k_buf.shape
    views, specs = [], []
    for buf in (k_buf, v_buf):
        for d in B_DILATIONS:
            assert wb % (d * B_BLOCK) == 0
            views.append(buf.reshape(bsz, wb // d, d * d_b))
            last = wb // d // B_BLOCK - 1
            specs.append(pl.BlockSpec((1, B_BLOCK, d_b), lambda b, last=last: (b, last, 0)))
    zspec = lambda c: pl.BlockSpec((1, 1, d_b), lambda b: (b, 0, c))
    n_pat = len(B_DILATIONS)
    return pl.pallas_call(
        _dil_sample_kernel,
        grid=(bsz,),
        in_specs=[zspec(col0), zspec(col0 + 1), zspec(col0 + 2), zspec(col0 + 3)] + specs
                 + [pl.BlockSpec((n_pat, H_B, B_BLOCK), lambda b: (0, 0, 0)),
                    pl.BlockSpec((H_B, 1), lambda b: (0, 0))],
        out_specs=pl.BlockSpec((1, 1, d_b), lambda b: (b, 0, 0)),
        out_shape=jax.ShapeDtypeStruct((bsz, 1, d_b), BF16),
        compiler_params=_params("parallel"),
        name="dilated_sample",
    )(z3, z3, z3, z3, *views, bias, bias0)


def _shift_kernel(kb_ref, vb_ref, kn_ref, vn_ref, ko_ref, vo_ref, sem):
    bsz, wb = kb_ref.shape[0], kb_ref.shape[1]
    copies = []
    for i, (buf, new, out) in enumerate(((kb_ref, kn_ref, ko_ref), (vb_ref, vn_ref, vo_ref))):
        for b in range(bsz):
            copies.append(pltpu.make_async_copy(buf.at[b, pl.ds(1, wb - 1)], out.at[b, pl.ds(0, wb - 1)],
                                                sem.at[i * (bsz + 1) + b]))
        copies.append(pltpu.make_async_copy(new, out.at[:, pl.ds(wb - 1, 1)], sem.at[i * (bsz + 1) + bsz]))
    for c in copies:
        c.start()
    for c in copies:
        c.wait()


def _shift_caches(k_buf, v_buf, k_new, v_new):
    bsz = k_buf.shape[0]
    any_spec = pl.BlockSpec(memory_space=pl.ANY)
    return pl.pallas_call(
        _shift_kernel,
        in_specs=[any_spec] * 4,
        out_specs=[any_spec] * 2,
        out_shape=[jax.ShapeDtypeStruct(k_buf.shape, k_buf.dtype), jax.ShapeDtypeStruct(v_buf.shape, v_buf.dtype)],
        scratch_shapes=[pltpu.SemaphoreType.DMA((2 * (bsz + 1),))],
        name="shift_caches",
    )(k_buf, v_buf, k_new, v_new)


def _row_tile(m, cap):
    return m if m <= cap else cap


def _even_weights(w_in, w_out, conv_w, conv_b, gate_w, gate_b, lam, rel_bias):
    d_a = conv_w.shape[1]
    gw = jnp.concatenate([gate_w[0], gate_w[1]], axis=-1).astype(BF16)
    return dict(w_in=w_in.astype(BF16), wa=w_out[:d_a].astype(BF16), wb=w_out[d_a:].astype(BF16),
                conv_w=conv_w, conv_b=conv_b[None], gw=gw, gate_b=gate_b, lam=lam[None],
                bias_p=_dil_prompt_bias(rel_bias), bias_s=_dil_sample_bias(rel_bias), d_a=d_a)


def _even_prompt(x, g_pre, g_post, w):
    s = x.shape[0]
    d_a = w["d_a"]
    d_b = w["wb"].shape[0]
    z = _norm_proj(x, g_pre, w["w_in"], tm=_row_tile(s, 1024), tn=1024)
    ya, h_last, conv_new = _rglru_prompt(z, w["conv_w"], w["conv_b"], w["gw"], w["gate_b"], w["lam"],
                                         d_a=d_a, tb=256)
    yb = _dil_prompt(z, w["bias_p"], col0=2 * d_a // LANES, tile=B_DILATIONS[-1] * B_BLOCK)
    y = _out_proj(ya, yb, w["wa"], w["wb"], g_post, x, tm=_row_tile(s, 512))
    wbp = min(B_DILATIONS[-1] * B_BLOCK, s)
    k_state = z[s - wbp:, 2 * d_a + d_b:2 * d_a + 2 * d_b]
    v_state = z[s - wbp:, 2 * d_a + 2 * d_b:2 * d_a + 3 * d_b]
    return y, h_last, conv_new, k_state, v_state


def _even_sample(x, g_pre, g_post, w, h0, conv0, k_buf, v_buf):
    bsz = x.shape[0]
    d_a = w["d_a"]
    d_b = w["wb"].shape[0]
    wb = k_buf.shape[1]
    z = _norm_proj(x, g_pre, w["w_in"], tm=bsz, tn=1024)
    ya, h_new, conv_new_t = _rglru_sample(z, jnp.swapaxes(conv0, 0, 1), h0, w["conv_w"], w["conv_b"],
                                          w["gw"], w["gate_b"], w["lam"], d_a=d_a)
    bias_s, bias0 = w["bias_s"]
    yb = _dil_sample(z.reshape(bsz, 1, -1), k_buf.reshape(bsz, wb, d_b), v_buf.reshape(bsz, wb, d_b),
                     bias_s, bias0, col0=2 * d_a // d_b)
    k_new = z[:, 2 * d_a + d_b:2 * d_a + 2 * d_b].reshape(bsz, 1, *k_buf.shape[2:])
    v_new = z[:, 2 * d_a + 2 * d_b:2 * d_a + 3 * d_b].reshape(bsz, 1, *v_buf.shape[2:])
    k_out, v_out = _shift_caches(k_buf, v_buf, k_new, v_new)
    y = _out_proj(ya, yb.reshape(bsz, d_b), w["wa"], w["wb"], g_post, x, tm=bsz)
    return y, h_new, jnp.swapaxes(conv_new_t, 0, 1), k_out, v_out


def _dot_f32(a, b):
    return jnp.dot(a, b, precision=lax.Precision.HIGHEST, preferred_element_type=F32)


def _mlstm_prompt_kernel(q_ref, k_ref, v_ref, o_ref, g_ref, gates_ref, gbias_ref,
                         y_ref, c_out, n_out, m_out, c_s, n_s, m_s):
    lc = q_ref.shape[0]
    hd = q_ref.shape[1] // H_C
    kscale = hd ** -0.5

    @pl.when(pl.program_id(0) == 0)
    def _():
        c_s[...] = jnp.zeros_like(c_s)
        n_s[...] = jnp.zeros_like(n_s)
        m_s[...] = jnp.zeros_like(m_s)

    gpre = gates_ref[...] + gbias_ref[...]
    lane = lax.broadcasted_iota(jnp.int32, gpre.shape, 1)
    ti = lax.broadcasted_iota(jnp.int32, (lc, lc), 0)
    si = lax.broadcasted_iota(jnp.int32, (lc, lc), 1)
    causal = si <= ti
    bcum = _dot_f32(causal.astype(F32), _log_sigmoid(gpre))
    mix = jnp.where(lane < H_C, gpre, bcum)
    mix_t = mix.T

    for h in range(H_C):
        sl = slice(h * hd, (h + 1) * hd)
        i_col, b_col = mix[:, h:h + 1], mix[:, H_C + h:H_C + h + 1]
        i_row, b_row = mix_t[h:h + 1, :], mix_t[H_C + h:H_C + h + 1, :]
        m0 = m_s[h:h + 1, 0:1]
        q = q_ref[:, sl]
        qb = q.astype(BF16)
        ks = k_ref[:, sl] * kscale
        kb = ks.astype(BF16)
        v = v_ref[:, sl]
        c0 = c_s[h]
        n0 = n_s[h:h + 1, :]

        dm = jnp.where(causal, b_col - b_row + i_row, NEG_INF)
        inter = b_col + m0
        mt = jnp.maximum(inter, jnp.max(dm, axis=1, keepdims=True))
        wqk = jnp.exp(dm - mt) * _dot_nt(qb, kb)
        g = jnp.exp(inter - mt)
        num = _dot(wqk.astype(BF16), v.astype(BF16)) + g * _dot_nt(qb, c0.astype(BF16))
        den = jnp.sum(wqk, axis=1, keepdims=True) + g * jnp.sum(q * n0, axis=1, keepdims=True)
        hh = num / jnp.maximum(jnp.abs(den), jnp.exp(-mt))
        y_ref[:, sl] = (_sigmoid(o_ref[:, sl]) * hh * _silu(g_ref[:, sl])).astype(y_ref.dtype)

        b_last = b_col[lc - 1:lc, :]
        wlast = b_last - b_col + i_col
        m_new = jnp.maximum(b_last + m0, jnp.max(wlast, axis=0, keepdims=True))
        ws = jnp.exp(wlast - m_new)
        g_last = jnp.exp(b_last + m0 - m_new)
        c_new = g_last * c0 + _dot_tn((v * ws).astype(BF16), kb)
        n_new = g_last * n0 + jnp.sum(ws * ks, axis=0, keepdims=True)
        c_s[h] = c_new
        n_s[h:h + 1, :] = n_new
        m_s[h:h + 1, :] = jnp.broadcast_to(m_new, (1, LANES))
        c_out[h] = c_new
        n_out[h:h + 1, :] = n_new
    m_out[...] = m_s[...]


def _mlstm_prompt(z, gates, gbias, *, d_c):
    s = z.shape[0]
    lc = C_CHUNK if s % C_CHUNK == 0 else s
    hd = d_c // H_C
    full = lambda shape: pl.BlockSpec(shape, lambda c: (0,) * len(shape))
    sec = lambda j: pl.BlockSpec((lc, d_c), lambda c, j=j: (c, j))
    return pl.pallas_call(
        _mlstm_prompt_kernel,
        grid=(s // lc,),
        in_specs=[sec(0), sec(1), sec(2), sec(3), sec(4),
                  pl.BlockSpec((lc, LANES), lambda c: (c, 0)), full((1, LANES))],
        out_specs=[pl.BlockSpec((lc, d_c), lambda c: (c, 0)),
                   full((H_C, hd, hd)), full((H_C, hd)), full((8, LANES))],
        out_shape=[jax.ShapeDtypeStruct((s, d_c), BF16),
                   jax.ShapeDtypeStruct((H_C, hd, hd), F32),
                   jax.ShapeDtypeStruct((H_C, hd), F32),
                   jax.ShapeDtypeStruct((8, LANES), F32)],
        scratch_shapes=[pltpu.VMEM((H_C, hd, hd), F32), pltpu.VMEM((H_C, hd), F32),
                        pltpu.VMEM((8, LANES), F32)],
        compiler_params=_params("arbitrary"),
        name="mlstm_prompt",
    )(z, z, z, z, z, gates, gbias)


def _mlstm_sample_kernel(q_ref, k_ref, v_ref, o_ref, g_ref, gates_ref, gbias_ref, c_ref, n_ref, m_ref,
                         y_ref, c_out, n_out, m_out, lf_out):
    hd = q_ref.shape[-1] // H_C
    kscale = hd ** -0.5
    gpre = gates_ref[0] + gbias_ref[...]
    logf = _log_sigmoid(gpre)
    lf_out[0] = logf
    for h in range(H_C):
        sl = slice(h * hd, (h + 1) * hd)
        i_g = gpre[:, h:h + 1]
        f_g = logf[:, H_C + h:H_C + h + 1]
        m0 = m_ref[0, :, h:h + 1]
        q = q_ref[0, :, sl]
        ks = k_ref[0, :, sl] * kscale
        v = v_ref[0, :, sl]
        c0 = c_ref[0, h]
        n0 = n_ref[0, h:h + 1, :]

        inter = f_g + m0
        mt = jnp.maximum(inter, i_g)
        wqk = jnp.exp(i_g - mt) * jnp.sum(q * ks, axis=1, keepdims=True)
        g = jnp.exp(inter - mt)
        cq = _dot_nt(jnp.broadcast_to(q, (8, hd)).astype(BF16), c0.astype(BF16))[0:1, :]
        num = wqk * v + g * cq
        den = wqk + g * jnp.sum(n0 * q, axis=1, keepdims=True)
        hh = num / jnp.maximum(jnp.abs(den), jnp.exp(-mt))
        y_ref[0, :, sl] = (_sigmoid(o_ref[0, :, sl]) * hh * _silu(g_ref[0, :, sl])).astype(y_ref.dtype)

        m_new = jnp.maximum(f_g + m0, i_g)
        ws = jnp.exp(i_g - m_new)
        g_last = jnp.exp(f_g + m0 - m_new)
        v_col = jnp.broadcast_to(v, (LANES, hd)).T[:, 0:1]
        c_out[0, h] = g_last * c0 + ws * (v_col * ks)
        n_out[0, h:h + 1, :] = g_last * n0 + ws * ks
        m_out[0, :, h:h + 1] = m_new


def _mlstm_sample(z3, gates3, gbias, c0, n0, m0):
    bsz, n_h, hd = n0.shape
    d_c = n_h * hd
    sec = lambda j: pl.BlockSpec((1, 1, d_c), lambda b, j=j: (b, 0, j))
    return pl.pallas_call(
        _mlstm_sample_kernel,
        grid=(bsz,),
        in_specs=[sec(0), sec(1), sec(2), sec(3), sec(4),
                  pl.BlockSpec((1, 1, LANES), lambda b: (b, 0, 0)),
                  pl.BlockSpec((1, LANES), lambda b: (0, 0)),
                  pl.BlockSpec((1, n_h, hd, hd), lambda b: (b, 0, 0, 0)),
                  pl.BlockSpec((1, n_h, hd), lambda b: (b, 0, 0)),
                  pl.BlockSpec((1, 1, n_h), lambda b: (b, 0, 0))],
        out_specs=[pl.BlockSpec((1, 1, d_c), lambda b: (b, 0, 0)),
                   pl.BlockSpec((1, n_h, hd, hd), lambda b: (b, 0, 0, 0)),
                   pl.BlockSpec((1, n_h, hd), lambda b: (b, 0, 0)),
                   pl.BlockSpec((1, 1, n_h), lambda b: (b, 0, 0)),
                   pl.BlockSpec((1, 1, LANES), lambda b: (b, 0, 0))],
        out_shape=[jax.ShapeDtypeStruct((bsz, 1, d_c), BF16),
                   jax.ShapeDtypeStruct(c0.shape, F32),
                   jax.ShapeDtypeStruct(n0.shape, F32),
                   jax.ShapeDtypeStruct(m0.shape, F32),
                   jax.ShapeDtypeStruct((bsz, 1, LANES), F32)],
        compiler_params=_params("parallel"),
        name="mlstm_sample",
    )(z3, z3, z3, z3, z3, gates3, gbias, c0, n0, m0)


def _fox_gates_kernel(gates_ref, gbias_ref, logf_ref, ft_ref, carry_s):
    tb = gates_ref.shape[0]

    @pl.when(pl.program_id(0) == 0)
    def _():
        carry_s[...] = jnp.zeros_like(carry_s)

    logf = _log_sigmoid(gates_ref[...] + gbias_ref[...])
    logf_ref[...] = logf
    lf_t = logf.T[2 * H_C:2 * H_C + H_D, :]
    upper = (lax.broadcasted_iota(jnp.int32, (tb, tb), 0)
             <= lax.broadcasted_iota(jnp.int32, (tb, tb), 1)).astype(F32)
    cum = _dot_f32(lf_t, upper) + carry_s[:, 0:1]
    ft_ref[...] = cum
    carry_s[...] = jnp.broadcast_to(cum[:, tb - 1:tb], carry_s.shape)


def _fox_gates(gates, gbias, *, tb):
    s = gates.shape[0]
    return pl.pallas_call(
        _fox_gates_kernel,
        grid=(s // tb,),
        in_specs=[pl.BlockSpec((tb, LANES), lambda i: (i, 0)), pl.BlockSpec((1, LANES), lambda i: (0, 0))],
        out_specs=[pl.BlockSpec((tb, LANES), lambda i: (i, 0)), pl.BlockSpec((H_D, tb), lambda i: (0, i))],
        out_shape=[jax.ShapeDtypeStruct((s, LANES), F32), jax.ShapeDtypeStruct((H_D, s), F32)],
        scratch_shapes=[pltpu.VMEM((H_D, LANES), F32)],
        compiler_params=_params("arbitrary"),
        name="fox_gates",
    )(gates, gbias)


def _fox_prompt_kernel(q_ref, k_ref, v_ref, g_ref, f_ref, y_ref, m_s, l_s, acc_s):
    tq, hd = q_ref.shape
    i = pl.program_id(1)
    q = (q_ref[...] * hd ** -0.5).astype(BF16)
    m_s[...] = jnp.full_like(m_s, NEG_INF)
    l_s[...] = jnp.zeros_like(l_s)
    acc_s[...] = jnp.zeros_like(acc_s)

    def block(j, masked):
        start = pl.multiple_of(j * tq, tq)
        kb = k_ref[pl.ds(start, tq), :].astype(BF16)
        vb = v_ref[pl.ds(start, tq), :].astype(BF16)
        s = _dot_nt(q, kb) - f_ref[0, :, pl.ds(start, tq)]
        if masked:
            s = jnp.where(lax.broadcasted_iota(jnp.int32, s.shape, 1)
                          <= lax.broadcasted_iota(jnp.int32, s.shape, 0), s, NEG_INF)
        m_o = m_s[...]
        m_n = jnp.maximum(m_o, jnp.max(s, axis=1, keepdims=True))
        al = jnp.exp(m_o - m_n)
        p = jnp.exp(s - m_n[:, 0:1])
        l_s[...] = l_s[...] * al + jnp.sum(p, axis=1, keepdims=True)
        acc_s[...] = acc_s[...] * al + _dot(p.astype(BF16), vb)
        m_s[...] = m_n

    def body(j, carry):
        block(j, False)
        return carry

    lax.fori_loop(0, i, body, 0)
    block(i, True)
    y_ref[...] = (acc_s[...] / l_s[...] * _silu(g_ref[...])).astype(y_ref.dtype)


def _fox_prompt(z, ft3, *, col0, tq):
    s = z.shape[0]
    hd = LANES
    qc, kc, vc, gc = (col0 + i * H_D for i in range(4))
    return pl.pallas_call(
        _fox_prompt_kernel,
        grid=(H_D, s // tq),
        in_specs=[pl.BlockSpec((tq, hd), lambda h, i: (i, qc + h)),
                  pl.BlockSpec((s, hd), lambda h, i: (0, kc + h)),
                  pl.BlockSpec((s, hd), lambda h, i: (0, vc + h)),
                  pl.BlockSpec((tq, hd), lambda h, i: (i, gc + h)),
                  pl.BlockSpec((1, 1, s), lambda h, i: (h, 0, 0))],
        out_specs=pl.BlockSpec((tq, hd), lambda h, i: (i, h)),
        out_shape=jax.ShapeDtypeStruct((s, H_D * hd), BF16),
        scratch_shapes=[pltpu.VMEM((tq, hd), F32)] * 3,
        compiler_params=_params("parallel", "parallel"),
        name="fox_prompt",
    )(z, z, z, z, ft3)


def _fox_sample_kernel(pt_ref, q_ref, kn_ref, vn_ref, g_ref, lfn_ref, fb_ref, *rest, pages_per_step):
    g_pages = pages_per_step
    k_refs = rest[:g_pages]
    v_refs = rest[g_pages:2 * g_pages]
    lf_refs = rest[2 * g_pages:3 * g_pages]
    y_ref, m_s, l_s, acc_s, carry_s = rest[3 * g_pages:]
    j = pl.program_id(1)
    hd = q_ref.shape[-1]
    page = k_refs[0].shape[1]
    scale = hd ** -0.5
    q = q_ref[0]
    qb = q.astype(BF16)
    row = lax.broadcasted_iota(jnp.int32, (H_D, page), 0)

    @pl.when(j == 0)
    def _():
        lf_new = _log_sigmoid(lfn_ref[0] + fb_ref[...])
        carry_s[...] = jnp.broadcast_to(lf_new, carry_s.shape)
        s_new = jnp.sum(q * kn_ref[0], axis=1, keepdims=True) * scale
        m_s[...] = jnp.broadcast_to(s_new, m_s.shape)
        l_s[...] = jnp.ones_like(l_s)
        acc_s[...] = vn_ref[0]

    later = (lax.broadcasted_iota(jnp.int32, (page, page), 0)
             > lax.broadcasted_iota(jnp.int32, (page, page), 1)).astype(F32)
    carry = carry_s[:, 0:1]
    scores = []
    for i in range(g_pages):
        lf = lf_refs[i][0]
        bias = _dot_f32(lf, later) + carry
        carry = carry + jnp.sum(lf, axis=1, keepdims=True)
        s = jnp.zeros((H_D, page), F32)
        for h in range(H_D):
            s = jnp.where(row == h, _dot_nt(qb, k_refs[i][0, :, h, :].astype(BF16)), s)
        scores.append(s * scale + bias)
    carry_s[...] = jnp.broadcast_to(carry, carry_s.shape)

    m_o = m_s[:, 0:1]
    m_n = m_o
    for s in scores:
        m_n = jnp.maximum(m_n, jnp.max(s, axis=1, keepdims=True))
    al = jnp.exp(m_o - m_n)
    l_n = l_s[:, 0:1] * al
    acc = acc_s[...] * al
    for i in range(g_pages):
        p = jnp.exp(scores[i] - m_n)
        l_n = l_n + jnp.sum(p, axis=1, keepdims=True)
        pb = p.astype(BF16)
        for h in range(H_D):
            acc = acc + jnp.where(row == h, _dot(pb, v_refs[i][0, :, h, :].astype(BF16)), 0.0)
    m_s[...] = jnp.broadcast_to(m_n, m_s.shape)
    l_s[...] = jnp.broadcast_to(l_n, l_s.shape)
    acc_s[...] = acc

    @pl.when(j == pl.num_programs(1) - 1)
    def _():
        y_ref[0] = (acc / l_n * _silu(g_ref[0])).astype(y_ref.dtype)


def _fox_sample(page_table, zh, lfn, fbias, k_pool, v_pool, lf_pool_t, *, row0, pages_per_step):
    bsz, n_pages = page_table.shape
    _, page, n_h, hd = k_pool.shape
    g_pages = pages_per_step
    assert n_pages % g_pages == 0
    zspec = lambda c: pl.BlockSpec((1, n_h, hd), lambda b, j, pt: (b, c, 0))

    def page_of(i):
        return lambda b, j, pt: pt[b, n_pages - 1 - (j * g_pages + i)]

    kv_specs = [pl.BlockSpec((1, page, n_h, hd), lambda b, j, pt, f=page_of(i): (f(b, j, pt), 0, 0, 0))
                for i in range(g_pages)]
    lf_specs = [pl.BlockSpec((1, n_h, page), lambda b, j, pt, f=page_of(i): (f(b, j, pt), 0, 0))
                for i in range(g_pages)]
    grid_spec = pltpu.PrefetchScalarGridSpec(
        num_scalar_prefetch=1,
        grid=(bsz, n_pages // g_pages),
        in_specs=[zspec(row0), zspec(row0 + 1), zspec(row0 + 2), zspec(row0 + 3),
                  pl.BlockSpec((1, n_h, 1), lambda b, j, pt: (b, 0, 0)),
                  pl.BlockSpec((n_h, 1), lambda b, j, pt: (0, 0))] + kv_specs + kv_specs + lf_specs,
        out_specs=pl.BlockSpec((1, n_h, hd), lambda b, j, pt: (b, 0, 0)),
        scratch_shapes=[pltpu.VMEM((n_h, hd), F32)] * 4,
    )
    return pl.pallas_call(
        functools.partial(_fox_sample_kernel, pages_per_step=g_pages),
        grid_spec=grid_spec,
        out_shape=jax.ShapeDtypeStruct((bsz, n_h, hd), BF16),
        compiler_params=_params("parallel", "arbitrary"),
        name="fox_sample",
    )(page_table, zh, zh, zh, zh, lfn, fbias, *([k_pool] * g_pages), *([v_pool] * g_pages),
      *([lf_pool_t] * g_pages))


def _odd_weights(w_in, w_out, c_gate_b, d_f_b):
    d_d = H_D * LANES
    d_c = w_out.shape[0] - d_d
    g0 = 5 * d_c
    g1 = g0 + 2 * H_C + 4 * d_d
    d = w_in.shape[0]
    w_main = jnp.concatenate([w_in[:, :g0], w_in[:, g0 + 2 * H_C:g1]], axis=1).astype(BF16)
    n_g = 2 * H_C + H_D
    w_gate = jnp.concatenate([w_in[:, g0:g0 + 2 * H_C], w_in[:, g1:g1 + H_D],
                              jnp.zeros((d, LANES - n_g), w_in.dtype)], axis=1).astype(BF16)
    gbias = jnp.concatenate([c_gate_b[0], c_gate_b[1], d_f_b, jnp.zeros((LANES - n_g,), F32)])[None]
    return dict(w_in=w_main, w_gate=w_gate, gbias=gbias, fbias=d_f_b[:, None],
                wa=w_out[:d_c].astype(BF16), wb=w_out[d_c:].astype(BF16), d_c=d_c, d_d=d_d)


def _odd_prompt(x, g_pre, g_post, w):
    s = x.shape[0]
    d_c, d_d = w["d_c"], w["d_d"]
    z, gates = _norm_proj(x, g_pre, w["w_in"], w["w_gate"], tm=_row_tile(s, 1024), tn=1024)
    yc, c_new, n_new, m8 = _mlstm_prompt(z, gates, w["gbias"], d_c=d_c)
    logf, ft = _fox_gates(gates, w["gbias"], tb=_row_tile(s, 512))
    yd = _fox_prompt(z, ft.reshape(H_D, 1, s), col0=5 * d_c // LANES, tq=_row_tile(s, 512))
    y = _out_proj(yc, yd, w["wa"], w["wb"], g_post, x, tm=_row_tile(s, 512))
    k_d = z[:, 5 * d_c + d_d:5 * d_c + 2 * d_d]
    v_d = z[:, 5 * d_c + 2 * d_d:5 * d_c + 3 * d_d]
    return y, c_new, n_new, m8[:H_C, 0], k_d, v_d, logf[:, 2 * H_C:2 * H_C + H_D]


def _odd_sample(x, g_pre, g_post, w, c0, n0, m0, k_pool, v_pool, lf_pool, page_table):
    bsz = x.shape[0]
    d_c, d_d = w["d_c"], w["d_d"]
    z, gates = _norm_proj(x, g_pre, w["w_in"], w["w_gate"], tm=bsz, tn=1024)
    z3 = z.reshape(bsz, 1, -1)
    yc, c_new, n_new, m_new, lf_row = _mlstm_sample(z3, gates.reshape(bsz, 1, LANES), w["gbias"],
                                                     c0, n0, m0.reshape(bsz, 1, H_C))
    lfn = gates[:, 2 * H_C:2 * H_C + H_D].reshape(bsz, H_D, 1)
    yd = _fox_sample(page_table, z.reshape(bsz, -1, LANES), lfn, w["fbias"], k_pool, v_pool,
                     jnp.swapaxes(lf_pool, 1, 2), row0=5 * d_c // d_d, pages_per_step=8)
    y = _out_proj(yc.reshape(bsz, d_c), yd.reshape(bsz, d_d), w["wa"], w["wb"], g_post, x, tm=bsz)
    k_d = z[:, 5 * d_c + d_d:5 * d_c + 2 * d_d]
    v_d = z[:, 5 * d_c + 2 * d_d:5 * d_c + 3 * d_d]
    return y, c_new, n_new, m_new.reshape(bsz, H_C), k_d, v_d, lf_row[:, 0, 2 * H_C:2 * H_C + H_D]


def kernel(x_prompt, x_sample, state_a_h, state_a_conv, cache_b_k, cache_b_v, state_c_C, state_c_n,
           state_c_m, cache_d_k, cache_d_v, cache_d_logf, page_table, norm_pre, norm_post, w_in_even,
           w_out_even, a_conv_w, a_conv_b, a_gate_w, a_gate_b, a_lambda, rel_bias, w_in_odd, w_out_odd,
           c_gate_b, d_f_b):
    bp, s, _ = x_prompt.shape
    bs = x_sample.shape[0]
    assert x_sample.shape[1] == 1
    xp = [x_prompt[b] for b in range(bp)]
    xs = x_sample[:, 0]
    names = ("ah", "ac", "bk", "bv", "cC", "cn", "cm", "dk", "dv", "dl")
    outs_p = {n: [] for n in names}
    outs_s = {n: [] for n in names}
    hb = cache_b_k.shape[3:]
    hd = cache_d_k.shape[3:]
    for l in range(norm_pre.shape[0]):
        j = l // 2
        g_pre, g_post = norm_pre[l][None], norm_post[l][None]
        if l % 2 == 0:
            w = _even_weights(w_in_even[j], w_out_even[j], a_conv_w[j], a_conv_b[j], a_gate_w[j],
                              a_gate_b[j], a_lambda[j], rel_bias)
            res = [_even_prompt(x, g_pre, g_post, w) for x in xp]
            xp = [r[0] for r in res]
            outs_p["ah"].append(jnp.concatenate([r[1] for r in res], axis=0))
            outs_p["ac"].append(jnp.stack([r[2] for r in res]))
            outs_p["bk"].append(jnp.stack([r[3].reshape(-1, *hb) for r in res]))
            outs_p["bv"].append(jnp.stack([r[4].reshape(-1, *hb) for r in res]))
            xs, h_new, conv_new, k_out, v_out = _even_sample(
                xs, g_pre, g_post, w, state_a_h[j], state_a_conv[j], cache_b_k[j], cache_b_v[j])
            outs_s["ah"].append(h_new)
            outs_s["ac"].append(conv_new)
            outs_s["bk"].append(k_out)
            outs_s["bv"].append(v_out)
        else:
            w = _odd_weights(w_in_odd[j], w_out_odd[j], c_gate_b[j], d_f_b[j])
            res = [_odd_prompt(x, g_pre, g_post, w) for x in xp]
            xp = [r[0] for r in res]
            outs_p["cC"].append(jnp.stack([r[1] for r in res]))
            outs_p["cn"].append(jnp.stack([r[2] for r in res]))
            outs_p["cm"].append(jnp.stack([r[3] for r in res]))
            outs_p["dk"].append(jnp.stack([r[4].reshape(s, *hd) for r in res]))
            outs_p["dv"].append(jnp.stack([r[5].reshape(s, *hd) for r in res]))
            outs_p["dl"].append(jnp.stack([r[6] for r in res]))
            xs, c_new, n_new, m_new, k_d, v_d, lf_d = _odd_sample(
                xs, g_pre, g_post, w, state_c_C[j], state_c_n[j], state_c_m[j],
                cache_d_k[j], cache_d_v[j], cache_d_logf[j], page_table)
            outs_s["cC"].append(c_new)
            outs_s["cn"].append(n_new)
            outs_s["cm"].append(m_new)
            outs_s["dk"].append(k_d.reshape(bs, 1, *hd))
            outs_s["dv"].append(v_d.reshape(bs, 1, *hd))
            outs_s["dl"].append(lf_d.reshape(bs, 1, -1))
    st = jnp.stack
    return (st(xp), xs[:, None, :],
            st(outs_p["ah"]), st(outs_s["ah"]), st(outs_p["ac"]), st(outs_s["ac"]),
            st(outs_p["bk"]), st(outs_p["bv"]), st(outs_s["bk"]), st(outs_s["bv"]),
            st(outs_p["cC"]), st(outs_p["cn"]), st(outs_p["cm"]),
            st(outs_s["cC"]), st(outs_s["cn"]), st(outs_s["cm"]),
            st(outs_p["dk"]), st(outs_p["dv"]), st(outs_p["dl"]),
            st(outs_s["dk"]), st(outs_s["dv"]), st(outs_s["dl"]))
```

```python
import functools
import math

import numpy as np
import jax
import jax.numpy as jnp
from jax import lax
from jax.experimental import pallas as pl
from jax.experimental.pallas import tpu as pltpu

F32 = jnp.float32
BF16 = jnp.bfloat16

LANES = 128
VMEM_LIMIT_BYTES = 56 * 2**20

RMS_EPS = 1e-6
LRU_C = 8.0
A_BLOCKS = 8
CONV_W = 4
H_B = 8
B_BLOCK = 128
B_DILATIONS = (1, 4, 16)
N_BUCKETS = 32
REL_MAX_DIST = 2048
H_C = 4
C_CHUNK = 128
H_D = 8
PAGE_SIZE = 128
NEG_INF = float("-inf")
LOG2E = math.log2(math.e)


def _params(*semantics):
    return pltpu.CompilerParams(dimension_semantics=semantics, vmem_limit_bytes=VMEM_LIMIT_BYTES)


def _sigmoid(x):
    return 1.0 / (1.0 + jnp.exp(-x))


def _silu(x):
    return x * _sigmoid(x)


def _softplus(x):
    return jnp.maximum(x, 0.0) + jnp.log1p(jnp.exp(-jnp.abs(x)))


def _log_sigmoid(x):
    return -_softplus(-x)


def _dot(a, b):
    return jnp.dot(a, b, preferred_element_type=F32)


def _dot_nt(a, b):
    return lax.dot_general(a, b, (((1,), (1,)), ((), ())), preferred_element_type=F32)


def _dot_tn(a, b):
    return lax.dot_general(a, b, (((0,), (0,)), ((), ())), preferred_element_type=F32)


def _norm_proj_kernel(x_ref, g_ref, w_ref, *rest, with_gates):
    if with_gates:
        wg_ref, o_ref, og_ref, hn_ref = rest
    else:
        o_ref, hn_ref = rest

    @pl.when(pl.program_id(1) == 0)
    def _():
        x = x_ref[...]
        ms = jnp.mean(x * x, axis=-1, keepdims=True)
        hn = (x * lax.rsqrt(ms + RMS_EPS) * g_ref[...]).astype(BF16)
        hn_ref[...] = hn
        if with_gates:
            og_ref[...] = _dot(hn, wg_ref[...])

    o_ref[...] = _dot(hn_ref[...], w_ref[...])


def _norm_proj(x, g, w, wg=None, *, tm, tn):
    m, d = x.shape
    n = w.shape[1]
    with_gates = wg is not None
    in_specs = [pl.BlockSpec((tm, d), lambda i, j: (i, 0)),
                pl.BlockSpec((1, d), lambda i, j: (0, 0)),
                pl.BlockSpec((d, tn), lambda i, j: (0, j))]
    out_specs = [pl.BlockSpec((tm, tn), lambda i, j: (i, j))]
    out_shape = [jax.ShapeDtypeStruct((m, n), F32)]
    args = [x, g, w]
    if with_gates:
        in_specs.append(pl.BlockSpec((d, LANES), lambda i, j: (0, 0)))
        out_specs.append(pl.BlockSpec((tm, LANES), lambda i, j: (i, 0)))
        out_shape.append(jax.ShapeDtypeStruct((m, LANES), F32))
        args.append(wg)
    outs = pl.pallas_call(
        functools.partial(_norm_proj_kernel, with_gates=with_gates),
        grid=(m // tm, n // tn),
        in_specs=in_specs, out_specs=out_specs, out_shape=out_shape,
        scratch_shapes=[pltpu.VMEM((tm, d), BF16)],
        compiler_params=_params("parallel", "arbitrary"),
        name="norm_proj",
    )(*args)
    return outs if with_gates else outs[0]


def _out_proj_kernel(ya_ref, yb_ref, wa_ref, wb_ref, g_ref, x_ref, o_ref):
    y = _dot(ya_ref[...], wa_ref[...]) + _dot(yb_ref[...], wb_ref[...])
    ms = jnp.mean(y * y, axis=-1, keepdims=True)
    o_ref[...] = x_ref[...] + y * lax.rsqrt(ms + RMS_EPS) * g_ref[...]


def _out_proj(ya, yb, wa, wb, g, x, *, tm):
    m, d = x.shape
    ka, kb = ya.shape[1], yb.shape[1]
    return pl.pallas_call(
        _out_proj_kernel,
        grid=(m // tm,),
        in_specs=[pl.BlockSpec((tm, ka), lambda i: (i, 0)),
                  pl.BlockSpec((tm, kb), lambda i: (i, 0)),
                  pl.BlockSpec((ka, d), lambda i: (0, 0)),
                  pl.BlockSpec((kb, d), lambda i: (0, 0)),
                  pl.BlockSpec((1, d), lambda i: (0, 0)),
                  pl.BlockSpec((tm, d), lambda i: (i, 0))],
        out_specs=pl.BlockSpec((tm, d), lambda i: (i, 0)),
        out_shape=jax.ShapeDtypeStruct((m, d), F32),
        compiler_params=_params("parallel"),
        name="out_proj",
    )(ya, yb, wa, wb, g, x)


def _rglru_gates(xc, ga_w_ref, gate_b_ref, sp_lam):
    d_a = xc.shape[1]
    bw = d_a // A_BLOCKS
    pre_r, pre_i = [], []
    for n in range(A_BLOCKS):
        pre = _dot(xc[:, n * bw:(n + 1) * bw].astype(BF16), ga_w_ref[n])
        pre_r.append(pre[:, :bw])
        pre_i.append(pre[:, bw:])
    r = _sigmoid(jnp.concatenate(pre_r, axis=1) + gate_b_ref[0:1, :])
    i = _sigmoid(jnp.concatenate(pre_i, axis=1) + gate_b_ref[1:2, :])
    log_a = -LRU_C * r * sp_lam
    a = jnp.exp(log_a)
    b = jnp.sqrt(-jnp.tanh(log_a) * (a * a + 1.0)) * i * xc
    return a, b


def _rglru_prompt_kernel(xa_ref, ga_ref, cw_ref, cb_ref, gw_ref, gb_ref, lam_ref,
                         ya_ref, hlast_ref, conv_ref, h_s, xprev_s):
    step = pl.program_id(0)
    tb = xa_ref.shape[0]

    @pl.when(step == 0)
    def _():
        h_s[...] = jnp.zeros_like(h_s)
        xprev_s[...] = jnp.zeros_like(xprev_s)

    xa = xa_ref[...]
    xcat = jnp.concatenate([xprev_s[...], xa], axis=0)
    xc = (cw_ref[0:1, :] * xcat[5:5 + tb] + cw_ref[1:2, :] * xcat[6:6 + tb]
          + cw_ref[2:3, :] * xcat[7:7 + tb] + cw_ref[3:4, :] * xa) + cb_ref[...]
    a, b = _rglru_gates(xc, gw_ref, gb_ref, _softplus(-lam_ref[...]))

    row = lax.broadcasted_iota(jnp.int32, a.shape, 0)
    s = 1
    while s < tb:
        keep = row >= s
        a_sh = jnp.where(keep, pltpu.roll(a, s, axis=0), 1.0)
        b_sh = jnp.where(keep, pltpu.roll(b, s, axis=0), 0.0)
        b = a * b_sh + b
        a = a * a_sh
        s *= 2
    h = a * h_s[...] + b
    ya_ref[...] = (h * _silu(ga_ref[...])).astype(ya_ref.dtype)
    h_s[...] = h[tb - 1:tb, :]
    hlast_ref[...] = h[tb - 1:tb, :]
    conv_ref[...] = xa[tb - (CONV_W - 1):tb, :]
    xprev_s[...] = xa[tb - 8:tb, :]


def _rglru_prompt(z, conv_w, conv_b, gw, gate_b, lam, *, d_a, tb):
    s = z.shape[0]
    bw = d_a // A_BLOCKS
    full = lambda shape: pl.BlockSpec(shape, lambda i: (0,) * len(shape))
    return pl.pallas_call(
        _rglru_prompt_kernel,
        grid=(s // tb,),
        in_specs=[pl.BlockSpec((tb, d_a), lambda i: (i, 0)),
                  pl.BlockSpec((tb, d_a), lambda i: (i, 1)),
                  full((CONV_W, d_a)), full((1, d_a)), full((A_BLOCKS, bw, 2 * bw)),
                  full((2, d_a)), full((1, d_a))],
        out_specs=[pl.BlockSpec((tb, d_a), lambda i: (i, 0)),
                   full((1, d_a)), full((CONV_W - 1, d_a))],
        out_shape=[jax.ShapeDtypeStruct((s, d_a), BF16),
                   jax.ShapeDtypeStruct((1, d_a), F32),
                   jax.ShapeDtypeStruct((CONV_W - 1, d_a), F32)],
        scratch_shapes=[pltpu.VMEM((1, d_a), F32), pltpu.VMEM((8, d_a), F32)],
        compiler_params=_params("arbitrary"),
        name="rglru_prompt",
    )(z, z, conv_w, conv_b, gw, gate_b, lam)


def _rglru_sample_kernel(xa_ref, ga_ref, c0_ref, h0_ref, cw_ref, cb_ref, gw_ref, gb_ref, lam_ref,
                         ya_ref, h_ref, conv_ref):
    xa = xa_ref[...]
    xc = (cw_ref[0:1, :] * c0_ref[0] + cw_ref[1:2, :] * c0_ref[1]
          + cw_ref[2:3, :] * c0_ref[2] + cw_ref[3:4, :] * xa) + cb_ref[...]
    a, b = _rglru_gates(xc, gw_ref, gb_ref, _softplus(-lam_ref[...]))
    h = a * h0_ref[...] + b
    ya_ref[...] = (h * _silu(ga_ref[...])).astype(ya_ref.dtype)
    h_ref[...] = h
    conv_ref[0] = c0_ref[1]
    conv_ref[1] = c0_ref[2]
    conv_ref[2] = xa


def _rglru_sample(z, conv0_t, h0, conv_w, conv_b, gw, gate_b, lam, *, d_a):
    bsz = z.shape[0]
    bw = d_a // A_BLOCKS
    full = lambda shape: pl.BlockSpec(shape, lambda i: (0,) * len(shape))
    return pl.pallas_call(
        _rglru_sample_kernel,
        grid=(1,),
        in_specs=[pl.BlockSpec((bsz, d_a), lambda i: (0, 0)),
                  pl.BlockSpec((bsz, d_a), lambda i: (0, 1)),
                  full((CONV_W - 1, bsz, d_a)), full((bsz, d_a)),
                  full((CONV_W, d_a)), full((1, d_a)), full((A_BLOCKS, bw, 2 * bw)),
                  full((2, d_a)), full((1, d_a))],
        out_specs=[full((bsz, d_a)), full((bsz, d_a)), full((CONV_W - 1, bsz, d_a))],
        out_shape=[jax.ShapeDtypeStruct((bsz, d_a), BF16),
                   jax.ShapeDtypeStruct((bsz, d_a), F32),
                   jax.ShapeDtypeStruct((CONV_W - 1, bsz, d_a), F32)],
        compiler_params=_params("arbitrary"),
        name="rglru_sample",
    )(z, z, conv0_t, h0, conv_w, conv_b, gw, gate_b, lam)


def _t5_bucket(dist):
    exact = N_BUCKETS // 2
    n = np.maximum(dist, 1).astype(np.float32)
    large = exact + (np.log(n / np.float32(exact)) / np.float32(math.log(REL_MAX_DIST / exact))
                     * np.float32(N_BUCKETS - exact)).astype(np.int32)
    return np.where(dist < exact, dist, np.minimum(large, N_BUCKETS - 1))


def _dil_prompt_buckets():
    qi = np.arange(B_BLOCK)[:, None]
    ki = np.arange(2 * B_BLOCK)[None, :]
    rel = qi + B_BLOCK - ki
    valid = (rel >= 0) & (rel <= B_BLOCK)
    return np.stack([np.where(valid, _t5_bucket(np.maximum(rel, 0) * dil), -1)
                     for dil in B_DILATIONS]).astype(np.int32)


def _dil_prompt_kernel(q_ref, kc_ref, kp_ref, vc_ref, vp_ref, gb_ref, bucket_ref, rb_ref, o_ref,
                       m_s, l_s, acc_s, bias_ref):
    tile = q_ref.shape[0]
    hd = q_ref.shape[1]
    scale = hd ** -0.5
    head = pl.program_id(0)
    not_first = pl.program_id(1) > 0

    @pl.when(pl.program_id(1) == 0)
    def _():
        for p_idx in range(len(B_DILATIONS)):
            bucket = bucket_ref[p_idx]
            bias = jnp.full(bucket.shape, NEG_INF, F32)
            for b in range(N_BUCKETS):
                bias = jnp.where(bucket == b, rb_ref[b, head], bias)
            bias_ref[p_idx] = bias

    def rows(start, d):
        return pl.ds(start, B_BLOCK, stride=d) if d > 1 else pl.ds(start, B_BLOCK)

    def attend(p_idx, d, q_start, prev_k, prev_v, prev_start, cur_start, prev_is_other_tile):
        r = rows(q_start, d)
        q = (q_ref[r, :] * scale).astype(BF16)
        kp = prev_k[rows(prev_start, d), :].astype(BF16)
        vp = prev_v[rows(prev_start, d), :].astype(BF16)
        kc = kc_ref[rows(cur_start, d), :].astype(BF16)
        vc = vc_ref[rows(cur_start, d), :].astype(BF16)
        bias = bias_ref[p_idx]
        s_p = _dot_nt(q, kp) + bias[:, :B_BLOCK]
        s_c = _dot_nt(q, kc) + bias[:, B_BLOCK:]
        if prev_is_other_tile:
            s_p = jnp.where(not_first, s_p, NEG_INF)
        m_b = jnp.maximum(jnp.max(s_p, axis=1, keepdims=True), jnp.max(s_c, axis=1, keepdims=True))
        p_p = jnp.exp(s_p - m_b)
        p_c = jnp.exp(s_c - m_b)
        den = jnp.sum(p_p, axis=1, keepdims=True) + jnp.sum(p_c, axis=1, keepdims=True)
        o = _dot(p_p.astype(BF16), vp) + _dot(p_c.astype(BF16), vc)
        m_b = jnp.broadcast_to(m_b, (B_BLOCK, hd))
        den = jnp.broadcast_to(den, (B_BLOCK, hd))
        if p_idx == 0:
            m_s[r, :] = m_b
            l_s[r, :] = den
            acc_s[r, :] = o
        else:
            m_o = m_s[r, :]
            m_n = jnp.maximum(m_o, m_b)
            al = jnp.exp(m_o - m_n)
            be = jnp.exp(m_b - m_n)
            m_s[r, :] = m_n
            l_s[r, :] = l_s[r, :] * al + den * be
            acc_s[r, :] = acc_s[r, :] * al + o * be

    for p_idx, d in enumerate(B_DILATIONS):
        unit = d * B_BLOCK
        n_units = tile // unit
        if d == 1:
            attend(p_idx, 1, 0, kp_ref, vp_ref, tile - B_BLOCK, 0, True)

            def body(j, carry):
                cur = pl.multiple_of(j * B_BLOCK, B_BLOCK)
                prev = pl.multiple_of((j - 1) * B_BLOCK, B_BLOCK)
                attend(0, 1, cur, kc_ref, vc_ref, prev, cur, False)
                return carry

            lax.fori_loop(1, n_units, body, 0)
        else:
            for u in range(n_units):
                for c in range(d):
                    if u == 0:
                        attend(p_idx, d, c, kp_ref, vp_ref, tile - unit + c, c, True)
                    else:
                        attend(p_idx, d, u * unit + c, kc_ref, vc_ref, (u - 1) * unit + c, u * unit + c, False)

    o_ref[...] = (acc_s[...] / l_s[...] * _silu(gb_ref[...])).astype(o_ref.dtype)


def _dil_prompt(z, rel_bias, *, col0, tile):
    s = z.shape[0]
    hd = LANES
    n_pat = len(B_DILATIONS)
    qc, kc, vc, gc = (col0 + i * H_B for i in range(4))
    prev = lambda t: jnp.maximum(t - 1, 0)
    blk = lambda shape, fn: pl.BlockSpec(shape, fn)
    return pl.pallas_call(
        _dil_prompt_kernel,
        grid=(H_B, s // tile),
        in_specs=[blk((tile, hd), lambda h, t: (t, qc + h)),
                  blk((tile, hd), lambda h, t: (t, kc + h)),
                  blk((tile, hd), lambda h, t: (prev(t), kc + h)),
                  blk((tile, hd), lambda h, t: (t, vc + h)),
                  blk((tile, hd), lambda h, t: (prev(t), vc + h)),
                  blk((tile, hd), lambda h, t: (t, gc + h)),
                  blk((n_pat, B_BLOCK, 2 * B_BLOCK), lambda h, t: (0, 0, 0)),
                  pl.BlockSpec(memory_space=pltpu.SMEM)],
        out_specs=blk((tile, hd), lambda h, t: (t, h)),
        out_shape=jax.ShapeDtypeStruct((s, H_B * hd), BF16),
        scratch_shapes=[pltpu.VMEM((tile, hd), F32)] * 3 + [pltpu.VMEM((n_pat, B_BLOCK, 2 * B_BLOCK), F32)],
        compiler_params=_params("parallel", "arbitrary"),
        name="dilated_prompt",
    )(z, z, z, z, z, z, jnp.asarray(_dil_prompt_buckets()), rel_bias)


def _dil_sample_kernel(q_ref, kn_ref, vn_ref, gb_ref, *rest):
    n_pat = len(B_DILATIONS)
    k_refs = rest[:n_pat]
    v_refs = rest[n_pat:2 * n_pat]
    bias_ref, bias0_ref, o_ref = rest[2 * n_pat:]
    n_h, hd = q_ref.shape[1:]
    q = q_ref[0] * hd ** -0.5
    qb = q.astype(BF16)
    s_new = jnp.sum(q * kn_ref[0], axis=1, keepdims=True) + bias0_ref[...]
    s = [_dot_nt(qb, k_refs[g][0].reshape(B_BLOCK * n_h, hd).astype(BF16)) + bias_ref[g]
         for g in range(n_pat)]
    m = s_new
    for sg in s:
        m = jnp.maximum(m, jnp.max(sg, axis=1, keepdims=True))
    p_new = n_pat * jnp.exp(s_new - m)
    den = p_new
    num = p_new * vn_ref[0]
    for g in range(n_pat):
        p = jnp.exp(s[g] - m)
        den = den + jnp.sum(p, axis=1, keepdims=True)
        num = num + _dot(p.astype(BF16), v_refs[g][0].reshape(B_BLOCK * n_h, hd).astype(BF16))
    o_ref[0] = (num / den * _silu(gb_ref[0])).astype(o_ref.dtype)


def _dil_sample_bias(rel_bias):
    n_h = rel_bias.shape[1]
    j = np.arange(B_BLOCK)
    own = (np.arange(B_BLOCK * n_h)[None, :] % n_h) == np.arange(n_h)[:, None]
    tabs = []
    for dil in B_DILATIONS:
        flat = rel_bias[_t5_bucket((B_BLOCK - j) * dil)].reshape(1, B_BLOCK * n_h)
        tabs.append(jnp.where(own, flat, NEG_INF))
    return jnp.stack(tabs), rel_bias[_t5_bucket(np.zeros((1,), np.int64))].T


def _dil_sample(zh, k_buf, v_buf, bias, bias0, *, row0):
    bsz, wb, n_h, hd = k_buf.shape
    views, specs = [], []
    for buf in (k_buf, v_buf):
        for d in B_DILATIONS:
            assert wb % (d * B_BLOCK) == 0
            views.append(buf.reshape(bsz, wb // d, d * n_h, hd))
            last = wb // d // B_BLOCK - 1
            specs.append(pl.BlockSpec((1, B_BLOCK, n_h, hd), lambda b, last=last: (b, last, 0, 0)))
    zspec = lambda c: pl.BlockSpec((1, n_h, hd), lambda b: (b, c, 0))
    n_pat = len(B_DILATIONS)
    return pl.pallas_call(
        _dil_sample_kernel,
        grid=(bsz,),
        in_specs=[zspec(row0), zspec(row0 + 1), zspec(row0 + 2), zspec(row0 + 3)] + specs
                 + [pl.BlockSpec((n_pat, n_h, B_BLOCK * n_h), lambda b: (0, 0, 0)),
                    pl.BlockSpec((n_h, 1), lambda b: (0, 0))],
        out_specs=pl.BlockSpec((1, n_h, hd), lambda b: (b, 0, 0)),
        out_shape=jax.ShapeDtypeStruct((bsz, n_h, hd), BF16),
        compiler_params=_params("parallel"),
        name="dilated_sample",
    )(zh, zh, zh, zh, *views, bias, bias0)


SHIFT_SLOTS = 4
SHIFT_ROWS = 1024


def _shift_kernel(kb_ref, vb_ref, kn_ref, vn_ref, ko_ref, vo_ref, stage, new_stage, in_sem, out_sem, new_sem):
    bsz, wb = kb_ref.shape[0], kb_ref.shape[1]
    n_slots, rows = stage.shape[0], stage.shape[1]

    new_in = [pltpu.make_async_copy(new, new_stage.at[i], new_sem.at[i])
              for i, new in enumerate((kn_ref, vn_ref))]
    new_out = [pltpu.make_async_copy(new_stage.at[i], out.at[:, pl.ds(wb - 1, 1)], new_sem.at[i])
               for i, out in enumerate((ko_ref, vo_ref))]
    for c in new_in:
        c.start()

    chunks = []
    for buf, out in ((kb_ref, ko_ref), (vb_ref, vo_ref)):
        for b in range(bsz):
            r = 1
            while r < wb:
                n = min(rows, wb - r)
                chunks.append((buf, out, b, r, n))
                r += n

    def copy_in(idx):
        buf, _, b, r, n = chunks[idx]
        slot = idx % n_slots
        return pltpu.make_async_copy(buf.at[b, pl.ds(r, n)], stage.at[slot, pl.ds(0, n)], in_sem.at[slot])

    def copy_out(idx):
        _, out, b, r, n = chunks[idx]
        slot = idx % n_slots
        return pltpu.make_async_copy(stage.at[slot, pl.ds(0, n)], out.at[b, pl.ds(r - 1, n)], out_sem.at[slot])

    ahead = n_slots - 1
    for idx in range(min(ahead, len(chunks))):
        copy_in(idx).start()
    for idx in range(len(chunks)):
        copy_in(idx).wait()
        copy_out(idx).start()
        if idx >= 1:
            copy_out(idx - 1).wait()
        if idx + ahead < len(chunks):
            copy_in(idx + ahead).start()
    copy_out(len(chunks) - 1).wait()

    for c in new_in:
        c.wait()
    for c in new_out:
        c.start()
    for c in new_out:
        c.wait()


def _shift_caches(k_buf, v_buf, k_new, v_new):
    bsz, wb = k_buf.shape[:2]
    rows = min(SHIFT_ROWS, wb)
    any_spec = pl.BlockSpec(memory_space=pl.ANY)
    return pl.pallas_call(
        _shift_kernel,
        in_specs=[any_spec] * 4,
        out_specs=[any_spec] * 2,
        out_shape=[jax.ShapeDtypeStruct(k_buf.shape, k_buf.dtype), jax.ShapeDtypeStruct(v_buf.shape, v_buf.dtype)],
        scratch_shapes=[pltpu.VMEM((SHIFT_SLOTS, rows) + k_buf.shape[2:], k_buf.dtype),
                        pltpu.VMEM((2,) + k_new.shape, k_new.dtype),
                        pltpu.SemaphoreType.DMA((SHIFT_SLOTS,)),
                        pltpu.SemaphoreType.DMA((SHIFT_SLOTS,)),
                        pltpu.SemaphoreType.DMA((2,))],
        compiler_params=pltpu.CompilerParams(vmem_limit_bytes=VMEM_LIMIT_BYTES),
        name="shift_caches",
    )(k_buf, v_buf, k_new, v_new)


def _row_tile(m, cap):
    return m if m <= cap else cap


def _even_weights(w_in, w_out, conv_w, conv_b, gate_w, gate_b, lam, rel_bias):
    d_a = conv_w.shape[1]
    gw = jnp.concatenate([gate_w[0], gate_w[1]], axis=-1).astype(BF16)
    return dict(w_in=w_in.astype(BF16), wa=w_out[:d_a].astype(BF16), wb=w_out[d_a:].astype(BF16),
                conv_w=conv_w, conv_b=conv_b[None], gw=gw, gate_b=gate_b, lam=lam[None],
                rel_bias=rel_bias, bias_s=_dil_sample_bias(rel_bias), d_a=d_a)


def _even_prompt(x, g_pre, g_post, w):
    s = x.shape[0]
    d_a = w["d_a"]
    d_b = w["wb"].shape[0]
    z = _norm_proj(x, g_pre, w["w_in"], tm=_row_tile(s, 1024), tn=1024)
    ya, h_last, conv_new = _rglru_prompt(z, w["conv_w"], w["conv_b"], w["gw"], w["gate_b"], w["lam"],
                                         d_a=d_a, tb=256)
    yb = _dil_prompt(z, w["rel_bias"], col0=2 * d_a // LANES, tile=B_DILATIONS[-1] * B_BLOCK)
    y = _out_proj(ya, yb, w["wa"], w["wb"], g_post, x, tm=_row_tile(s, 512))
    wbp = min(B_DILATIONS[-1] * B_BLOCK, s)
    k_state = z[s - wbp:, 2 * d_a + d_b:2 * d_a + 2 * d_b]
    v_state = z[s - wbp:, 2 * d_a + 2 * d_b:2 * d_a + 3 * d_b]
    return y, h_last, conv_new, k_state, v_state


def _even_sample(x, g_pre, g_post, w, h0, conv0, k_buf, v_buf):
    bsz = x.shape[0]
    d_a = w["d_a"]
    d_b = w["wb"].shape[0]
    wb = k_buf.shape[1]
    z = _norm_proj(x, g_pre, w["w_in"], tm=bsz, tn=1024)
    ya, h_new, conv_new_t = _rglru_sample(z, jnp.swapaxes(conv0, 0, 1), h0, w["conv_w"], w["conv_b"],
                                          w["gw"], w["gate_b"], w["lam"], d_a=d_a)
    bias_s, bias0 = w["bias_s"]
    yb = _dil_sample(z.reshape(bsz, -1, k_buf.shape[-1]), k_buf, v_buf, bias_s, bias0, row0=2 * d_a // d_b)
    k_new = z[:, 2 * d_a + d_b:2 * d_a + 2 * d_b].reshape(bsz, 1, *k_buf.shape[2:])
    v_new = z[:, 2 * d_a + 2 * d_b:2 * d_a + 3 * d_b].reshape(bsz, 1, *v_buf.shape[2:])
    k_out, v_out = _shift_caches(k_buf, v_buf, k_new, v_new)
    y = _out_proj(ya, yb.reshape(bsz, d_b), w["wa"], w["wb"], g_post, x, tm=bsz)
    return y, h_new, jnp.swapaxes(conv_new_t, 0, 1), k_out, v_out


def _dot_f32(a, b):
    return jnp.dot(a, b, precision=lax.Precision.HIGHEST, preferred_element_type=F32)


def _mlstm_prompt_kernel(q_ref, k_ref, v_ref, o_ref, g_ref, gates_ref, gbias_ref,
                         y_ref, c_out, n_out, m_out, c_s, n_s, m_s):
    lc = q_ref.shape[0]
    hd = q_ref.shape[1] // H_C
    kscale = hd ** -0.5

    @pl.when(pl.program_id(0) == 0)
    def _():
        c_s[...] = jnp.zeros_like(c_s)
        n_s[...] = jnp.zeros_like(n_s)
        m_s[...] = jnp.zeros_like(m_s)

    gpre = gates_ref[...] + gbias_ref[...]
    lane = lax.broadcasted_iota(jnp.int32, gpre.shape, 1)
    ti = lax.broadcasted_iota(jnp.int32, (lc, lc), 0)
    si = lax.broadcasted_iota(jnp.int32, (lc, lc), 1)
    causal = si <= ti
    bcum = _dot_f32(causal.astype(F32), _log_sigmoid(gpre))
    mix = jnp.where(lane < H_C, gpre, bcum)
    mix_t = mix.T

    for h in range(H_C):
        sl = slice(h * hd, (h + 1) * hd)
        i_col, b_col = mix[:, h:h + 1], mix[:, H_C + h:H_C + h + 1]
        i_row, b_row = mix_t[h:h + 1, :], mix_t[H_C + h:H_C + h + 1, :]
        m0 = m_s[h:h + 1, 0:1]
        q = q_ref[:, sl]
        qb = q.astype(BF16)
        ks = k_ref[:, sl] * kscale
        kb = ks.astype(BF16)
        v = v_ref[:, sl]
        c0 = c_s[h]
        n0 = n_s[h:h + 1, :]

        dm = jnp.where(causal, b_col - b_row + i_row, NEG_INF)
        inter = b_col + m0
        mt = jnp.maximum(inter, jnp.max(dm, axis=1, keepdims=True))
        wqk = jnp.exp(dm - mt) * _dot_nt(qb, kb)
        g = jnp.exp(inter - mt)
        num = _dot(wqk.astype(BF16), v.astype(BF16)) + g * _dot_nt(qb, c0.astype(BF16))
        den = jnp.sum(wqk, axis=1, keepdims=True) + g * jnp.sum(q * n0, axis=1, keepdims=True)
        hh = num / jnp.maximum(jnp.abs(den), jnp.exp(-mt))
        y_ref[:, sl] = (_sigmoid(o_ref[:, sl]) * hh * _silu(g_ref[:, sl])).astype(y_ref.dtype)

        b_last = b_col[lc - 1:lc, :]
        wlast = b_last - b_col + i_col
        m_new = jnp.maximum(b_last + m0, jnp.max(wlast, axis=0, keepdims=True))
        ws = jnp.exp(wlast - m_new)
        g_last = jnp.exp(b_last + m0 - m_new)
        c_new = g_last * c0 + _dot_tn((v * ws).astype(BF16), kb)
        n_new = g_last * n0 + jnp.sum(ws * ks, axis=0, keepdims=True)
        c_s[h] = c_new
        n_s[h:h + 1, :] = n_new
        m_s[h:h + 1, :] = jnp.broadcast_to(m_new, (1, LANES))
        c_out[h] = c_new
        n_out[h:h + 1, :] = n_new
    m_out[...] = m_s[...]


def _mlstm_prompt(z, gates, gbias, *, d_c):
    s = z.shape[0]
    lc = C_CHUNK if s % C_CHUNK == 0 else s
    hd = d_c // H_C
    full = lambda shape: pl.BlockSpec(shape, lambda c: (0,) * len(shape))
    sec = lambda j: pl.BlockSpec((lc, d_c), lambda c, j=j: (c, j))
    return pl.pallas_call(
        _mlstm_prompt_kernel,
        grid=(s // lc,),
        in_specs=[sec(0), sec(1), sec(2), sec(3), sec(4),
                  pl.BlockSpec((lc, LANES), lambda c: (c, 0)), full((1, LANES))],
        out_specs=[pl.BlockSpec((lc, d_c), lambda c: (c, 0)),
                   full((H_C, hd, hd)), full((H_C, hd)), full((8, LANES))],
        out_shape=[jax.ShapeDtypeStruct((s, d_c), BF16),
                   jax.ShapeDtypeStruct((H_C, hd, hd), F32),
                   jax.ShapeDtypeStruct((H_C, hd), F32),
                   jax.ShapeDtypeStruct((8, LANES), F32)],
        scratch_shapes=[pltpu.VMEM((H_C, hd, hd), F32), pltpu.VMEM((H_C, hd), F32),
                        pltpu.VMEM((8, LANES), F32)],
        compiler_params=_params("arbitrary"),
        name="mlstm_prompt",
    )(z, z, z, z, z, gates, gbias)


def _mlstm_sample_kernel(q_ref, k_ref, v_ref, o_ref, g_ref, gates_ref, gbias_ref, c_ref, n_ref, m_ref,
                         y_ref, c_out, n_out, m_out, lf_out):
    hd = q_ref.shape[-1] // H_C
    kscale = hd ** -0.5
    gpre = gates_ref[0] + gbias_ref[...]
    logf = _log_sigmoid(gpre)
    lf_out[0] = logf
    for h in range(H_C):
        sl = slice(h * hd, (h + 1) * hd)
        i_g = gpre[:, h:h + 1]
        f_g = logf[:, H_C + h:H_C + h + 1]
        m0 = m_ref[0, :, h:h + 1]
        q = q_ref[0, :, sl]
        ks = k_ref[0, :, sl] * kscale
        v = v_ref[0, :, sl]
        c0 = c_ref[0, h]
        n0 = n_ref[0, h:h + 1, :]

        inter = f_g + m0
        mt = jnp.maximum(inter, i_g)
        wqk = jnp.exp(i_g - mt) * jnp.sum(q * ks, axis=1, keepdims=True)
        g = jnp.exp(inter - mt)
        cq = _dot_nt(jnp.broadcast_to(q, (8, hd)).astype(BF16), c0.astype(BF16))[0:1, :]
        num = wqk * v + g * cq
        den = wqk + g * jnp.sum(n0 * q, axis=1, keepdims=True)
        hh = num / jnp.maximum(jnp.abs(den), jnp.exp(-mt))
        y_ref[0, :, sl] = (_sigmoid(o_ref[0, :, sl]) * hh * _silu(g_ref[0, :, sl])).astype(y_ref.dtype)

        m_new = jnp.maximum(f_g + m0, i_g)
        ws = jnp.exp(i_g - m_new)
        g_last = jnp.exp(f_g + m0 - m_new)
        v_col = jnp.broadcast_to(v, (LANES, hd)).T[:, 0:1]
        c_out[0, h] = g_last * c0 + ws * (v_col * ks)
        n_out[0, h:h + 1, :] = g_last * n0 + ws * ks
        m_out[0, :, h:h + 1] = m_new


def _mlstm_sample(z3, gates3, gbias, c0, n0, m0):
    bsz, n_h, hd = n0.shape
    d_c = n_h * hd
    sec = lambda j: pl.BlockSpec((1, 1, d_c), lambda b, j=j: (b, 0, j))
    return pl.pallas_call(
        _mlstm_sample_kernel,
        grid=(bsz,),
        in_specs=[sec(0), sec(1), sec(2), sec(3), sec(4),
                  pl.BlockSpec((1, 1, LANES), lambda b: (b, 0, 0)),
                  pl.BlockSpec((1, LANES), lambda b: (0, 0)),
                  pl.BlockSpec((1, n_h, hd, hd), lambda b: (b, 0, 0, 0)),
                  pl.BlockSpec((1, n_h, hd), lambda b: (b, 0, 0)),
                  pl.BlockSpec((1, 1, n_h), lambda b: (b, 0, 0))],
        out_specs=[pl.BlockSpec((1, 1, d_c), lambda b: (b, 0, 0)),
                   pl.BlockSpec((1, n_h, hd, hd), lambda b: (b, 0, 0, 0)),
                   pl.BlockSpec((1, n_h, hd), lambda b: (b, 0, 0)),
                   pl.BlockSpec((1, 1, n_h), lambda b: (b, 0, 0)),
                   pl.BlockSpec((1, 1, LANES), lambda b: (b, 0, 0))],
        out_shape=[jax.ShapeDtypeStruct((bsz, 1, d_c), BF16),
                   jax.ShapeDtypeStruct(c0.shape, F32),
                   jax.ShapeDtypeStruct(n0.shape, F32),
                   jax.ShapeDtypeStruct(m0.shape, F32),
                   jax.ShapeDtypeStruct((bsz, 1, LANES), F32)],
        compiler_params=_params("parallel"),
        name="mlstm_sample",
    )(z3, z3, z3, z3, z3, gates3, gbias, c0, n0, m0)


def _fox_gates_kernel(gates_ref, gbias_ref, logf_ref, ft_ref, carry_s):
    tb = gates_ref.shape[0]

    @pl.when(pl.program_id(0) == 0)
    def _():
        carry_s[...] = jnp.zeros_like(carry_s)

    logf = _log_sigmoid(gates_ref[...] + gbias_ref[...])
    logf_ref[...] = logf
    lf_t = logf.T[2 * H_C:2 * H_C + H_D, :]
    upper = (lax.broadcasted_iota(jnp.int32, (tb, tb), 0)
             <= lax.broadcasted_iota(jnp.int32, (tb, tb), 1)).astype(F32)
    cum = _dot_f32(lf_t, upper) + carry_s[:, 0:1]
    ft_ref[...] = cum
    carry_s[...] = jnp.broadcast_to(cum[:, tb - 1:tb], carry_s.shape)


def _fox_gates(gates, gbias, *, tb):
    s = gates.shape[0]
    return pl.pallas_call(
        _fox_gates_kernel,
        grid=(s // tb,),
        in_specs=[pl.BlockSpec((tb, LANES), lambda i: (i, 0)), pl.BlockSpec((1, LANES), lambda i: (0, 0))],
        out_specs=[pl.BlockSpec((tb, LANES), lambda i: (i, 0)), pl.BlockSpec((H_D, tb), lambda i: (0, i))],
        out_shape=[jax.ShapeDtypeStruct((s, LANES), F32), jax.ShapeDtypeStruct((H_D, s), F32)],
        scratch_shapes=[pltpu.VMEM((H_D, LANES), F32)],
        compiler_params=_params("arbitrary"),
        name="fox_gates",
    )(gates, gbias)


def _fox_prompt_kernel(q_ref, k_ref, v_ref, g_ref, f_ref, y_ref, m_s, l_s, acc_s):
    tq, hd = q_ref.shape
    i = pl.program_id(1)
    q = (q_ref[...] * (hd ** -0.5 * LOG2E)).astype(BF16)
    m_s[...] = jnp.full_like(m_s, NEG_INF)
    l_s[...] = jnp.zeros_like(l_s)
    acc_s[...] = jnp.zeros_like(acc_s)

    def block(j, masked):
        start = pl.multiple_of(j * tq, tq)
        kb = k_ref[pl.ds(start, tq), :].astype(BF16)
        vb = v_ref[pl.ds(start, tq), :].astype(BF16)
        s = _dot_nt(q, kb) - f_ref[0, :, pl.ds(start, tq)] * LOG2E
        if masked:
            s = jnp.where(lax.broadcasted_iota(jnp.int32, s.shape, 1)
                          <= lax.broadcasted_iota(jnp.int32, s.shape, 0), s, NEG_INF)
        m_o = m_s[...]
        m_n = jnp.maximum(m_o, jnp.max(s, axis=1, keepdims=True))
        al = jnp.exp2(m_o - m_n)
        p = jnp.exp2(s - m_n[:, 0:1])
        l_s[...] = l_s[...] * al + jnp.sum(p, axis=1, keepdims=True)
        acc_s[...] = acc_s[...] * al + _dot(p.astype(BF16), vb)
        m_s[...] = m_n

    def body(j, carry):
        block(j, False)
        return carry

    lax.fori_loop(0, i, body, 0)
    block(i, True)
    y_ref[...] = (acc_s[...] / l_s[...] * _silu(g_ref[...])).astype(y_ref.dtype)


def _fox_prompt(z, ft3, *, col0, tq):
    s = z.shape[0]
    hd = LANES
    qc, kc, vc, gc = (col0 + i * H_D for i in range(4))
    return pl.pallas_call(
        _fox_prompt_kernel,
        grid=(H_D, s // tq),
        in_specs=[pl.BlockSpec((tq, hd), lambda h, i: (i, qc + h)),
                  pl.BlockSpec((s, hd), lambda h, i: (0, kc + h)),
                  pl.BlockSpec((s, hd), lambda h, i: (0, vc + h)),
                  pl.BlockSpec((tq, hd), lambda h, i: (i, gc + h)),
                  pl.BlockSpec((1, 1, s), lambda h, i: (h, 0, 0))],
        out_specs=pl.BlockSpec((tq, hd), lambda h, i: (i, h)),
        out_shape=jax.ShapeDtypeStruct((s, H_D * hd), BF16),
        scratch_shapes=[pltpu.VMEM((tq, hd), F32)] * 3,
        compiler_params=_params("parallel", "parallel"),
        name="fox_prompt",
    )(z, z, z, z, ft3)


def _page_sums_kernel(lf_ref, suf_ref, tot_ref):
    x = lf_ref[...]
    width = x.shape[1]
    lane = lax.broadcasted_iota(jnp.int32, x.shape, 1)
    inc, cyc = x, x
    step = H_D
    while step < width:
        shifted = pltpu.roll(inc, width - step, axis=1)
        inc = inc + jnp.where(lane < width - step, shifted, 0.0)
        cyc = cyc + pltpu.roll(cyc, width - step, axis=1)
        step *= 2
    suf_ref[...] = inc - x
    tot_ref[...] = cyc


def _page_sums(lf_flat, *, rows):
    n_pool, width = lf_flat.shape
    spec = pl.BlockSpec((rows, width), lambda i: (i, 0))
    return pl.pallas_call(
        _page_sums_kernel,
        grid=(n_pool // rows,),
        in_specs=[spec], out_specs=[spec, spec],
        out_shape=[jax.ShapeDtypeStruct(lf_flat.shape, F32)] * 2,
        compiler_params=_params("parallel"),
        name="page_sums",
    )(lf_flat)


def _fox_sample_kernel(pt_ref, q_ref, kn_ref, vn_ref, g_ref, lfn_ref, fb_ref, *rest, pages_per_step):
    g_pages = pages_per_step
    k_refs = rest[:g_pages]
    v_refs = rest[g_pages:2 * g_pages]
    suf_refs = rest[2 * g_pages:3 * g_pages]
    tot_refs = rest[3 * g_pages:4 * g_pages]
    y_ref, m_s, l_s, acc_s, carry_s = rest[4 * g_pages:]
    j = pl.program_id(1)
    n_h, hd = q_ref.shape[1:]
    width = k_refs[0].shape[1]
    scale = hd ** -0.5
    q = q_ref[0] * scale
    qb = q.astype(BF16)
    own = (lax.broadcasted_iota(jnp.int32, (n_h, width), 1) % n_h
           == lax.broadcasted_iota(jnp.int32, (n_h, width), 0))

    @pl.when(j == 0)
    def _():
        carry_s[...] = _log_sigmoid(lfn_ref[0] + fb_ref[...])
        s_new = jnp.sum(q * kn_ref[0], axis=1, keepdims=True)
        m_s[...] = jnp.broadcast_to(s_new, m_s.shape)
        l_s[...] = jnp.ones_like(l_s)
        acc_s[...] = vn_ref[0]

    carry = carry_s[...]
    scores = []
    for i in range(g_pages):
        bias = suf_refs[i][0] + carry
        carry = carry + tot_refs[i][0]
        s = _dot_nt(qb, k_refs[i][0].astype(BF16)) + bias
        scores.append(jnp.where(own, s, NEG_INF))
    carry_s[...] = carry

    m_o = m_s[:, 0:1]
    m_n = m_o
    for s in scores:
        m_n = jnp.maximum(m_n, jnp.max(s, axis=1, keepdims=True))
    al = jnp.exp(m_o - m_n)
    l_n = l_s[:, 0:1] * al
    acc = acc_s[...] * al
    for i in range(g_pages):
        p = jnp.exp(scores[i] - m_n)
        l_n = l_n + jnp.sum(p, axis=1, keepdims=True)
        acc = acc + _dot(p.astype(BF16), v_refs[i][0].astype(BF16))
    m_s[...] = jnp.broadcast_to(m_n, m_s.shape)
    l_s[...] = jnp.broadcast_to(l_n, l_s.shape)
    acc_s[...] = acc

    @pl.when(j == pl.num_programs(1) - 1)
    def _():
        y_ref[0] = (acc / l_n * _silu(g_ref[0])).astype(y_ref.dtype)


def _fox_sample(page_table, zh, lfn_row, fbias_row, k_pool, v_pool, suf, tot, *, row0, pages_per_step):
    bsz, n_pages = page_table.shape
    _, width, hd = k_pool.shape
    n_h = H_D
    g_pages = pages_per_step
    assert n_pages % g_pages == 0
    zspec = lambda c: pl.BlockSpec((1, n_h, hd), lambda b, j, pt: (b, c, 0))

    def page_of(i):
        return lambda b, j, pt: pt[b, n_pages - 1 - (j * g_pages + i)]

    kv_specs = [pl.BlockSpec((1, width, hd), lambda b, j, pt, f=page_of(i): (f(b, j, pt), 0, 0))
                for i in range(g_pages)]
    row_specs = [pl.BlockSpec((1, 1, width), lambda b, j, pt, f=page_of(i): (f(b, j, pt), 0, 0))
                 for i in range(g_pages)]
    grid_spec = pltpu.PrefetchScalarGridSpec(
        num_scalar_prefetch=1,
        grid=(bsz, n_pages // g_pages),
        in_specs=[zspec(row0), zspec(row0 + 1), zspec(row0 + 2), zspec(row0 + 3),
                  pl.BlockSpec((1, 1, width), lambda b, j, pt: (b, 0, 0)),
                  pl.BlockSpec((1, width), lambda b, j, pt: (0, 0))]
                 + kv_specs + kv_specs + row_specs + row_specs,
        out_specs=pl.BlockSpec((1, n_h, hd), lambda b, j, pt: (b, 0, 0)),
        scratch_shapes=[pltpu.VMEM((n_h, hd), F32)] * 3 + [pltpu.VMEM((1, width), F32)],
    )
    return pl.pallas_call(
        functools.partial(_fox_sample_kernel, pages_per_step=g_pages),
        grid_spec=grid_spec,
        out_shape=jax.ShapeDtypeStruct((bsz, n_h, hd), BF16),
        compiler_params=_params("parallel", "arbitrary"),
        name="fox_sample",
    )(page_table, zh, zh, zh, zh, lfn_row, fbias_row, *([k_pool] * g_pages), *([v_pool] * g_pages),
      *([suf] * g_pages), *([tot] * g_pages))


def _odd_weights(w_in, w_out, c_gate_b, d_f_b):
    d_d = H_D * LANES
    d_c = w_out.shape[0] - d_d
    g0 = 5 * d_c
    g1 = g0 + 2 * H_C + 4 * d_d
    d = w_in.shape[0]
    w_main = jnp.concatenate([w_in[:, :g0], w_in[:, g0 + 2 * H_C:g1]], axis=1).astype(BF16)
    n_g = 2 * H_C + H_D
    w_gate = jnp.concatenate([w_in[:, g0:g0 + 2 * H_C], w_in[:, g1:g1 + H_D],
                              jnp.zeros((d, LANES - n_g), w_in.dtype)], axis=1).astype(BF16)
    gbias = jnp.concatenate([c_gate_b[0], c_gate_b[1], d_f_b, jnp.zeros((LANES - n_g,), F32)])[None]
    return dict(w_in=w_main, w_gate=w_gate, gbias=gbias, fbias=d_f_b[None, :],
                wa=w_out[:d_c].astype(BF16), wb=w_out[d_c:].astype(BF16), d_c=d_c, d_d=d_d)


def _odd_prompt(x, g_pre, g_post, w):
    s = x.shape[0]
    d_c, d_d = w["d_c"], w["d_d"]
    z, gates = _norm_proj(x, g_pre, w["w_in"], w["w_gate"], tm=_row_tile(s, 1024), tn=1024)
    yc, c_new, n_new, m8 = _mlstm_prompt(z, gates, w["gbias"], d_c=d_c)
    logf, ft = _fox_gates(gates, w["gbias"], tb=_row_tile(s, 512))
    yd = _fox_prompt(z, ft.reshape(H_D, 1, s), col0=5 * d_c // LANES, tq=_row_tile(s, 512))
    y = _out_proj(yc, yd, w["wa"], w["wb"], g_post, x, tm=_row_tile(s, 512))
    k_d = z[:, 5 * d_c + d_d:5 * d_c + 2 * d_d]
    v_d = z[:, 5 * d_c + 2 * d_d:5 * d_c + 3 * d_d]
    return y, c_new, n_new, m8[:H_C, 0], k_d, v_d, logf[:, 2 * H_C:2 * H_C + H_D]


def _odd_sample(x, g_pre, g_post, w, c0, n0, m0, k_pool, v_pool, lf_pool, page_table):
    bsz = x.shape[0]
    d_c, d_d = w["d_c"], w["d_d"]
    z, gates = _norm_proj(x, g_pre, w["w_in"], w["w_gate"], tm=bsz, tn=1024)
    z3 = z.reshape(bsz, 1, -1)
    yc, c_new, n_new, m_new, lf_row = _mlstm_sample(z3, gates.reshape(bsz, 1, LANES), w["gbias"],
                                                     c0, n0, m0.reshape(bsz, 1, H_C))
    n_pool, page, n_h, hd = k_pool.shape
    width = page * n_h
    suf, tot = _page_sums(lf_pool.reshape(n_pool, width), rows=math.gcd(n_pool, 256))
    lfn_row = jnp.tile(gates[:, 2 * H_C:2 * H_C + H_D], (1, page)).reshape(bsz, 1, width)
    yd = _fox_sample(page_table, z.reshape(bsz, -1, hd), lfn_row, jnp.tile(w["fbias"], (1, page)),
                     k_pool.reshape(n_pool, width, hd), v_pool.reshape(n_pool, width, hd),
                     suf.reshape(n_pool, 1, width), tot.reshape(n_pool, 1, width),
                     row0=5 * d_c // d_d, pages_per_step=8)
    y = _out_proj(yc.reshape(bsz, d_c), yd.reshape(bsz, d_d), w["wa"], w["wb"], g_post, x, tm=bsz)
    k_d = z[:, 5 * d_c + d_d:5 * d_c + 2 * d_d]
    v_d = z[:, 5 * d_c + 2 * d_d:5 * d_c + 3 * d_d]
    return y, c_new, n_new, m_new.reshape(bsz, H_C), k_d, v_d, lf_row[:, 0, 2 * H_C:2 * H_C + H_D]


def kernel(x_prompt, x_sample, state_a_h, state_a_conv, cache_b_k, cache_b_v, state_c_C, state_c_n,
           state_c_m, cache_d_k, cache_d_v, cache_d_logf, page_table, norm_pre, norm_post, w_in_even,
           w_out_even, a_conv_w, a_conv_b, a_gate_w, a_gate_b, a_lambda, rel_bias, w_in_odd, w_out_odd,
           c_gate_b, d_f_b):
    bp, s, _ = x_prompt.shape
    bs = x_sample.shape[0]
    assert x_sample.shape[1] == 1
    xp = [x_prompt[b] for b in range(bp)]
    xs = x_sample[:, 0]
    names = ("ah", "ac", "bk", "bv", "cC", "cn", "cm", "dk", "dv", "dl")
    outs_p = {n: [] for n in names}
    outs_s = {n: [] for n in names}
    hb = cache_b_k.shape[3:]
    hd = cache_d_k.shape[3:]
    for l in range(norm_pre.shape[0]):
        j = l // 2
        g_pre, g_post = norm_pre[l][None], norm_post[l][None]
        if l % 2 == 0:
            w = _even_weights(w_in_even[j], w_out_even[j], a_conv_w[j], a_conv_b[j], a_gate_w[j],
                              a_gate_b[j], a_lambda[j], rel_bias)
            res = [_even_prompt(x, g_pre, g_post, w) for x in xp]
            xp = [r[0] for r in res]
            outs_p["ah"].append(jnp.concatenate([r[1] for r in res], axis=0))
            outs_p["ac"].append(jnp.stack([r[2] for r in res]))
            outs_p["bk"].append(jnp.stack([r[3].reshape(-1, *hb) for r in res]))
            outs_p["bv"].append(jnp.stack([r[4].reshape(-1, *hb) for r in res]))
            xs, h_new, conv_new, k_out, v_out = _even_sample(
                xs, g_pre, g_post, w, state_a_h[j], state_a_conv[j], cache_b_k[j], cache_b_v[j])
            outs_s["ah"].append(h_new)
            outs_s["ac"].append(conv_new)
            outs_s["bk"].append(k_out)
            outs_s["bv"].append(v_out)
        else:
            w = _odd_weights(w_in_odd[j], w_out_odd[j], c_gate_b[j], d_f_b[j])
            res = [_odd_prompt(x, g_pre, g_post, w) for x in xp]
            xp = [r[0] for r in res]
            outs_p["cC"].append(jnp.stack([r[1] for r in res]))
            outs_p["cn"].append(jnp.stack([r[2] for r in res]))
            outs_p["cm"].append(jnp.stack([r[3] for r in res]))
            outs_p["dk"].append(jnp.stack([r[4].reshape(s, *hd) for r in res]))
            outs_p["dv"].append(jnp.stack([r[5].reshape(s, *hd) for r in res]))
            outs_p["dl"].append(jnp.stack([r[6] for r in res]))
            xs, c_new, n_new, m_new, k_d, v_d, lf_d = _odd_sample(
                xs, g_pre, g_post, w, state_c_C[j], state_c_n[j], state_c_m[j],
                cache_d_k[j], cache_d_v[j], cache_d_logf[j], page_table)
            outs_s["cC"].append(c_new)
            outs_s["cn"].append(n_new)
            outs_s["cm"].append(m_new)
            outs_s["dk"].append(k_d.reshape(bs, 1, *hd))
            outs_s["dv"].append(v_d.reshape(bs, 1, *hd))
            outs_s["dl"].append(lf_d.reshape(bs, 1, -1))
    st = jnp.stack
    return (st(xp), xs[:, None, :],
            st(outs_p["ah"]), st(outs_s["ah"]), st(outs_p["ac"]), st(outs_s["ac"]),
            st(outs_p["bk"]), st(outs_p["bv"]), st(outs_s["bk"]), st(outs_s["bv"]),
            st(outs_p["cC"]), st(outs_p["cn"]), st(outs_p["cm"]),
            st(outs_s["cC"]), st(outs_s["cn"]), st(outs_s["cm"]),
            st(outs_p["dk"]), st(outs_p["dv"]), st(outs_p["dl"]),
            st(outs_s["dk"]), st(outs_s["dv"]), st(outs_s["dl"]))
```

```python
import functools
import math

import numpy as np
import jax
import jax.numpy as jnp
from jax import lax
from jax.experimental import pallas as pl
from jax.experimental.pallas import tpu as pltpu

F32 = jnp.float32
BF16 = jnp.bfloat16

LANES = 128
VMEM_LIMIT_BYTES = 56 * 2**20

RMS_EPS = 1e-6
LRU_C = 8.0
A_BLOCKS = 8
CONV_W = 4
H_B = 8
B_BLOCK = 128
B_DILATIONS = (1, 4, 16)
N_BUCKETS = 32
REL_MAX_DIST = 2048
H_C = 4
C_CHUNK = 128
H_D = 8
PAGE_SIZE = 128
NEG_INF = float("-inf")
LOG2E = math.log2(math.e)


def _params(*semantics):
    return pltpu.CompilerParams(dimension_semantics=semantics, vmem_limit_bytes=VMEM_LIMIT_BYTES)


def _sigmoid(x):
    return 1.0 / (1.0 + jnp.exp(-x))


def _silu(x):
    return x * _sigmoid(x)


def _softplus(x):
    return jnp.maximum(x, 0.0) + jnp.log1p(jnp.exp(-jnp.abs(x)))


def _log_sigmoid(x):
    return -_softplus(-x)


def _dot(a, b):
    return jnp.dot(a, b, preferred_element_type=F32)


def _dot_nt(a, b):
    return lax.dot_general(a, b, (((1,), (1,)), ((), ())), preferred_element_type=F32)


def _dot_tn(a, b):
    return lax.dot_general(a, b, (((0,), (0,)), ((), ())), preferred_element_type=F32)


def _norm_proj_kernel(x_ref, g_ref, w_ref, *rest, with_gates):
    if with_gates:
        wg_ref, o_ref, og_ref, hn_ref = rest
    else:
        o_ref, hn_ref = rest

    @pl.when(pl.program_id(1) == 0)
    def _():
        x = x_ref[...]
        ms = jnp.mean(x * x, axis=-1, keepdims=True)
        hn = (x * lax.rsqrt(ms + RMS_EPS) * g_ref[...]).astype(BF16)
        hn_ref[...] = hn
        if with_gates:
            og_ref[...] = _dot(hn, wg_ref[...])

    o_ref[...] = _dot(hn_ref[...], w_ref[...])


def _norm_proj(x, g, w, wg=None, *, tm, tn):
    m, d = x.shape
    n = w.shape[1]
    with_gates = wg is not None
    in_specs = [pl.BlockSpec((tm, d), lambda i, j: (i, 0)),
                pl.BlockSpec((1, d), lambda i, j: (0, 0)),
                pl.BlockSpec((d, tn), lambda i, j: (0, j))]
    out_specs = [pl.BlockSpec((tm, tn), lambda i, j: (i, j))]
    out_shape = [jax.ShapeDtypeStruct((m, n), F32)]
    args = [x, g, w]
    if with_gates:
        in_specs.append(pl.BlockSpec((d, LANES), lambda i, j: (0, 0)))
        out_specs.append(pl.BlockSpec((tm, LANES), lambda i, j: (i, 0)))
        out_shape.append(jax.ShapeDtypeStruct((m, LANES), F32))
        args.append(wg)
    outs = pl.pallas_call(
        functools.partial(_norm_proj_kernel, with_gates=with_gates),
        grid=(m // tm, n // tn),
        in_specs=in_specs, out_specs=out_specs, out_shape=out_shape,
        scratch_shapes=[pltpu.VMEM((tm, d), BF16)],
        compiler_params=_params("parallel", "arbitrary"),
        name="norm_proj",
    )(*args)
    return outs if with_gates else outs[0]


def _out_proj_kernel(ya_ref, yb_ref, wa_ref, wb_ref, g_ref, x_ref, o_ref):
    y = _dot(ya_ref[...], wa_ref[...]) + _dot(yb_ref[...], wb_ref[...])
    ms = jnp.mean(y * y, axis=-1, keepdims=True)
    o_ref[...] = x_ref[...] + y * lax.rsqrt(ms + RMS_EPS) * g_ref[...]


def _out_proj(ya, yb, wa, wb, g, x, *, tm):
    m, d = x.shape
    ka, kb = ya.shape[1], yb.shape[1]
    return pl.pallas_call(
        _out_proj_kernel,
        grid=(m // tm,),
        in_specs=[pl.BlockSpec((tm, ka), lambda i: (i, 0)),
                  pl.BlockSpec((tm, kb), lambda i: (i, 0)),
                  pl.BlockSpec((ka, d), lambda i: (0, 0)),
                  pl.BlockSpec((kb, d), lambda i: (0, 0)),
                  pl.BlockSpec((1, d), lambda i: (0, 0)),
                  pl.BlockSpec((tm, d), lambda i: (i, 0))],
        out_specs=pl.BlockSpec((tm, d), lambda i: (i, 0)),
        out_shape=jax.ShapeDtypeStruct((m, d), F32),
        compiler_params=_params("parallel"),
        name="out_proj",
    )(ya, yb, wa, wb, g, x)


def _rglru_gates(xc, ga_w_ref, gate_b_ref, sp_lam):
    d_a = xc.shape[1]
    bw = d_a // A_BLOCKS
    pre_r, pre_i = [], []
    for n in range(A_BLOCKS):
        pre = _dot(xc[:, n * bw:(n + 1) * bw].astype(BF16), ga_w_ref[n])
        pre_r.append(pre[:, :bw])
        pre_i.append(pre[:, bw:])
    r = _sigmoid(jnp.concatenate(pre_r, axis=1) + gate_b_ref[0:1, :])
    i = _sigmoid(jnp.concatenate(pre_i, axis=1) + gate_b_ref[1:2, :])
    log_a = -LRU_C * r * sp_lam
    a = jnp.exp(log_a)
    b = jnp.sqrt(-jnp.tanh(log_a) * (a * a + 1.0)) * i * xc
    return a, b


def _rglru_prompt_kernel(xa_ref, ga_ref, cw_ref, cb_ref, gw_ref, gb_ref, lam_ref,
                         ya_ref, hlast_ref, conv_ref, h_s, xprev_s):
    step = pl.program_id(0)
    tb = xa_ref.shape[0]

    @pl.when(step == 0)
    def _():
        h_s[...] = jnp.zeros_like(h_s)
        xprev_s[...] = jnp.zeros_like(xprev_s)

    xa = xa_ref[...]
    xcat = jnp.concatenate([xprev_s[...], xa], axis=0)
    xc = (cw_ref[0:1, :] * xcat[5:5 + tb] + cw_ref[1:2, :] * xcat[6:6 + tb]
          + cw_ref[2:3, :] * xcat[7:7 + tb] + cw_ref[3:4, :] * xa) + cb_ref[...]
    a, b = _rglru_gates(xc, gw_ref, gb_ref, _softplus(-lam_ref[...]))

    row = lax.broadcasted_iota(jnp.int32, a.shape, 0)
    s = 1
    while s < tb:
        keep = row >= s
        a_sh = jnp.where(keep, pltpu.roll(a, s, axis=0), 1.0)
        b_sh = jnp.where(keep, pltpu.roll(b, s, axis=0), 0.0)
        b = a * b_sh + b
        a = a * a_sh
        s *= 2
    h = a * h_s[...] + b
    ya_ref[...] = (h * _silu(ga_ref[...])).astype(ya_ref.dtype)
    h_s[...] = h[tb - 1:tb, :]
    hlast_ref[...] = h[tb - 1:tb, :]
    conv_ref[...] = xa[tb - (CONV_W - 1):tb, :]
    xprev_s[...] = xa[tb - 8:tb, :]


def _rglru_prompt(z, conv_w, conv_b, gw, gate_b, lam, *, d_a, tb):
    s = z.shape[0]
    bw = d_a // A_BLOCKS
    full = lambda shape: pl.BlockSpec(shape, lambda i: (0,) * len(shape))
    return pl.pallas_call(
        _rglru_prompt_kernel,
        grid=(s // tb,),
        in_specs=[pl.BlockSpec((tb, d_a), lambda i: (i, 0)),
                  pl.BlockSpec((tb, d_a), lambda i: (i, 1)),
                  full((CONV_W, d_a)), full((1, d_a)), full((A_BLOCKS, bw, 2 * bw)),
                  full((2, d_a)), full((1, d_a))],
        out_specs=[pl.BlockSpec((tb, d_a), lambda i: (i, 0)),
                   full((1, d_a)), full((CONV_W - 1, d_a))],
        out_shape=[jax.ShapeDtypeStruct((s, d_a), BF16),
                   jax.ShapeDtypeStruct((1, d_a), F32),
                   jax.ShapeDtypeStruct((CONV_W - 1, d_a), F32)],
        scratch_shapes=[pltpu.VMEM((1, d_a), F32), pltpu.VMEM((8, d_a), F32)],
        compiler_params=_params("arbitrary"),
        name="rglru_prompt",
    )(z, z, conv_w, conv_b, gw, gate_b, lam)


def _rglru_sample_kernel(xa_ref, ga_ref, c0_ref, h0_ref, cw_ref, cb_ref, gw_ref, gb_ref, lam_ref,
                         ya_ref, h_ref, conv_ref):
    xa = xa_ref[...]
    xc = (cw_ref[0:1, :] * c0_ref[0] + cw_ref[1:2, :] * c0_ref[1]
          + cw_ref[2:3, :] * c0_ref[2] + cw_ref[3:4, :] * xa) + cb_ref[...]
    a, b = _rglru_gates(xc, gw_ref, gb_ref, _softplus(-lam_ref[...]))
    h = a * h0_ref[...] + b
    ya_ref[...] = (h * _silu(ga_ref[...])).astype(ya_ref.dtype)
    h_ref[...] = h
    conv_ref[0] = c0_ref[1]
    conv_ref[1] = c0_ref[2]
    conv_ref[2] = xa


def _rglru_sample(z, conv0_t, h0, conv_w, conv_b, gw, gate_b, lam, *, d_a):
    bsz = z.shape[0]
    bw = d_a // A_BLOCKS
    full = lambda shape: pl.BlockSpec(shape, lambda i: (0,) * len(shape))
    return pl.pallas_call(
        _rglru_sample_kernel,
        grid=(1,),
        in_specs=[pl.BlockSpec((bsz, d_a), lambda i: (0, 0)),
                  pl.BlockSpec((bsz, d_a), lambda i: (0, 1)),
                  full((CONV_W - 1, bsz, d_a)), full((bsz, d_a)),
                  full((CONV_W, d_a)), full((1, d_a)), full((A_BLOCKS, bw, 2 * bw)),
                  full((2, d_a)), full((1, d_a))],
        out_specs=[full((bsz, d_a)), full((bsz, d_a)), full((CONV_W - 1, bsz, d_a))],
        out_shape=[jax.ShapeDtypeStruct((bsz, d_a), BF16),
                   jax.ShapeDtypeStruct((bsz, d_a), F32),
                   jax.ShapeDtypeStruct((CONV_W - 1, bsz, d_a), F32)],
        compiler_params=_params("arbitrary"),
        name="rglru_sample",
    )(z, z, conv0_t, h0, conv_w, conv_b, gw, gate_b, lam)


def _t5_bucket(dist):
    exact = N_BUCKETS // 2
    n = np.maximum(dist, 1).astype(np.float32)
    large = exact + (np.log(n / np.float32(exact)) / np.float32(math.log(REL_MAX_DIST / exact))
                     * np.float32(N_BUCKETS - exact)).astype(np.int32)
    return np.where(dist < exact, dist, np.minimum(large, N_BUCKETS - 1))


def _dil_prompt_buckets():
    qi = np.arange(B_BLOCK)[:, None]
    ki = np.arange(2 * B_BLOCK)[None, :]
    rel = qi + B_BLOCK - ki
    valid = (rel >= 0) & (rel <= B_BLOCK)
    return np.stack([np.where(valid, _t5_bucket(np.maximum(rel, 0) * dil), -1)
                     for dil in B_DILATIONS]).astype(np.int32)


DIL_GROUP = 8


def _dil_prompt_kernel(q_ref, kc_ref, kp_ref, vc_ref, vp_ref, gb_ref, bucket_ref, rb_ref, o_ref,
                       bias_ref, *scratch):
    n_pat = len(B_DILATIONS)
    o_s = scratch[:n_pat]
    lse_s = scratch[n_pat:]
    tile, hd = q_ref.shape
    scale = hd ** -0.5 * LOG2E
    head = pl.program_id(0)
    not_first = pl.program_id(1) > 0

    @pl.when(pl.program_id(1) == 0)
    def _():
        for p_idx in range(n_pat):
            bucket = bucket_ref[p_idx]
            bias = jnp.full(bucket.shape, NEG_INF, F32)
            for b in range(N_BUCKETS):
                bias = jnp.where(bucket == b, rb_ref[b, head] * LOG2E, bias)
            bias_ref[p_idx] = bias

    def rows(start, n, d):
        return pl.ds(start, n, stride=d) if d > 1 else pl.ds(start, n)

    cur_half = lax.broadcasted_iota(jnp.int32, (1, B_BLOCK, 2 * B_BLOCK), 2) >= B_BLOCK

    for p_idx, d in enumerate(B_DILATIONS):
        unit = d * B_BLOCK
        blocks = [(u, c) for u in range(tile // unit) for c in range(d)]
        for g0 in range(0, len(blocks), DIL_GROUP):
            group = blocks[g0:g0 + DIL_GROUP]
            q_rows, qs, ks, vs = [], [], [], []
            for u, c in group:
                r = rows(u * unit + c, B_BLOCK, d)
                q_rows.append(r)
                qs.append(q_ref[r, :])
                if u == 0:
                    pr = rows(tile - unit + c, B_BLOCK, d)
                    cr = rows(c, B_BLOCK, d)
                    ks.append(jnp.concatenate([kp_ref[pr, :], kc_ref[cr, :]], axis=0))
                    vs.append(jnp.concatenate([vp_ref[pr, :], vc_ref[cr, :]], axis=0))
                else:
                    r2 = rows((u - 1) * unit + c, 2 * B_BLOCK, d)
                    ks.append(kc_ref[r2, :])
                    vs.append(vc_ref[r2, :])
            q3 = (jnp.stack(qs) * scale).astype(BF16)
            k3 = jnp.stack(ks).astype(BF16)
            v3 = jnp.stack(vs).astype(BF16)
            s3 = jnp.einsum("bqd,bkd->bqk", q3, k3, preferred_element_type=F32) + bias_ref[p_idx][None]
            n_first = sum(1 for u, _ in group if u == 0)
            if n_first:
                masked = jnp.where(jnp.logical_or(not_first, cur_half), s3[:n_first], NEG_INF)
                s3 = masked if n_first == len(group) else jnp.concatenate([masked, s3[n_first:]], axis=0)
            m3 = jnp.max(s3, axis=-1, keepdims=True)
            p3 = jnp.exp2(s3 - m3)
            l3 = jnp.sum(p3, axis=-1, keepdims=True)
            o3 = jnp.einsum("bqk,bkd->bqd", p3.astype(BF16), v3, preferred_element_type=F32) / l3
            lse3 = jnp.broadcast_to(m3 + jnp.log2(l3), o3.shape)
            for i, r in enumerate(q_rows):
                o_s[p_idx][r, :] = o3[i]
                lse_s[p_idx][r, :] = lse3[i]

    lses = [ref[...] for ref in lse_s]
    m = functools.reduce(jnp.maximum, lses)
    ws = [jnp.exp2(x - m) for x in lses]
    num = functools.reduce(lambda a, b: a + b, [w * ref[...] for w, ref in zip(ws, o_s)])
    den = functools.reduce(lambda a, b: a + b, ws)
    o_ref[...] = (num / den * _silu(gb_ref[...])).astype(o_ref.dtype)


def _dil_prompt(z, rel_bias, *, col0, tile):
    s = z.shape[0]
    hd = LANES
    n_pat = len(B_DILATIONS)
    qc, kc, vc, gc = (col0 + i * H_B for i in range(4))
    prev = lambda t: jnp.maximum(t - 1, 0)
    blk = lambda shape, fn: pl.BlockSpec(shape, fn)
    return pl.pallas_call(
        _dil_prompt_kernel,
        grid=(H_B, s // tile),
        in_specs=[blk((tile, hd), lambda h, t: (t, qc + h)),
                  blk((tile, hd), lambda h, t: (t, kc + h)),
                  blk((tile, hd), lambda h, t: (prev(t), kc + h)),
                  blk((tile, hd), lambda h, t: (t, vc + h)),
                  blk((tile, hd), lambda h, t: (prev(t), vc + h)),
                  blk((tile, hd), lambda h, t: (t, gc + h)),
                  blk((n_pat, B_BLOCK, 2 * B_BLOCK), lambda h, t: (0, 0, 0)),
                  pl.BlockSpec(memory_space=pltpu.SMEM)],
        out_specs=blk((tile, hd), lambda h, t: (t, h)),
        out_shape=jax.ShapeDtypeStruct((s, H_B * hd), BF16),
        scratch_shapes=[pltpu.VMEM((n_pat, B_BLOCK, 2 * B_BLOCK), F32)]
                       + [pltpu.VMEM((tile, hd), F32)] * (2 * n_pat),
        compiler_params=_params("parallel", "arbitrary"),
        name="dilated_prompt",
    )(z, z, z, z, z, z, jnp.asarray(_dil_prompt_buckets()), rel_bias)


def _dil_sample_kernel(q_ref, kn_ref, vn_ref, gb_ref, *rest):
    n_pat = len(B_DILATIONS)
    k_refs = rest[:n_pat]
    v_refs = rest[n_pat:2 * n_pat]
    bias_ref, bias0_ref, o_ref = rest[2 * n_pat:]
    n_h, hd = q_ref.shape[1:]
    q = q_ref[0] * hd ** -0.5
    qb = q.astype(BF16)
    s_new = jnp.sum(q * kn_ref[0], axis=1, keepdims=True) + bias0_ref[...]
    s = [_dot_nt(qb, k_refs[g][0].reshape(B_BLOCK * n_h, hd).astype(BF16)) + bias_ref[g]
         for g in range(n_pat)]
    m = s_new
    for sg in s:
        m = jnp.maximum(m, jnp.max(sg, axis=1, keepdims=True))
    p_new = n_pat * jnp.exp(s_new - m)
    den = p_new
    num = p_new * vn_ref[0]
    for g in range(n_pat):
        p = jnp.exp(s[g] - m)
        den = den + jnp.sum(p, axis=1, keepdims=True)
        num = num + _dot(p.astype(BF16), v_refs[g][0].reshape(B_BLOCK * n_h, hd).astype(BF16))
    o_ref[0] = (num / den * _silu(gb_ref[0])).astype(o_ref.dtype)


def _dil_sample_bias(rel_bias):
    n_h = rel_bias.shape[1]
    j = np.arange(B_BLOCK)
    own = (np.arange(B_BLOCK * n_h)[None, :] % n_h) == np.arange(n_h)[:, None]
    tabs = []
    for dil in B_DILATIONS:
        flat = rel_bias[_t5_bucket((B_BLOCK - j) * dil)].reshape(1, B_BLOCK * n_h)
        tabs.append(jnp.where(own, flat, NEG_INF))
    return jnp.stack(tabs), rel_bias[_t5_bucket(np.zeros((1,), np.int64))].T


def _dil_sample(zh, k_buf, v_buf, bias, bias0, *, row0):
    bsz, wb, n_h, hd = k_buf.shape
    views, specs = [], []
    for buf in (k_buf, v_buf):
        for d in B_DILATIONS:
            assert wb % (d * B_BLOCK) == 0
            views.append(buf.reshape(bsz, wb // d, d * n_h, hd))
            last = wb // d // B_BLOCK - 1
            specs.append(pl.BlockSpec((1, B_BLOCK, n_h, hd), lambda b, last=last: (b, last, 0, 0)))
    zspec = lambda c: pl.BlockSpec((1, n_h, hd), lambda b: (b, c, 0))
    n_pat = len(B_DILATIONS)
    return pl.pallas_call(
        _dil_sample_kernel,
        grid=(bsz,),
        in_specs=[zspec(row0), zspec(row0 + 1), zspec(row0 + 2), zspec(row0 + 3)] + specs
                 + [pl.BlockSpec((n_pat, n_h, B_BLOCK * n_h), lambda b: (0, 0, 0)),
                    pl.BlockSpec((n_h, 1), lambda b: (0, 0))],
        out_specs=pl.BlockSpec((1, n_h, hd), lambda b: (b, 0, 0)),
        out_shape=jax.ShapeDtypeStruct((bsz, n_h, hd), BF16),
        compiler_params=_params("parallel"),
        name="dilated_sample",
    )(zh, zh, zh, zh, *views, bias, bias0)


SHIFT_SLOTS = 4
SHIFT_ROWS = 1024


def _shift_kernel(kb_ref, vb_ref, kn_ref, vn_ref, ko_ref, vo_ref, stage, new_stage, in_sem, out_sem, new_sem):
    bsz, wb = kb_ref.shape[0], kb_ref.shape[1]
    n_slots, rows = stage.shape[0], stage.shape[1]

    new_in = [pltpu.make_async_copy(new, new_stage.at[i], new_sem.at[i])
              for i, new in enumerate((kn_ref, vn_ref))]
    new_out = [pltpu.make_async_copy(new_stage.at[i], out.at[:, pl.ds(wb - 1, 1)], new_sem.at[i])
               for i, out in enumerate((ko_ref, vo_ref))]
    for c in new_in:
        c.start()

    chunks = []
    for buf, out in ((kb_ref, ko_ref), (vb_ref, vo_ref)):
        for b in range(bsz):
            r = 1
            while r < wb:
                n = min(rows, wb - r)
                chunks.append((buf, out, b, r, n))
                r += n

    def copy_in(idx):
        buf, _, b, r, n = chunks[idx]
        slot = idx % n_slots
        return pltpu.make_async_copy(buf.at[b, pl.ds(r, n)], stage.at[slot, pl.ds(0, n)], in_sem.at[slot])

    def copy_out(idx):
        _, out, b, r, n = chunks[idx]
        slot = idx % n_slots
        return pltpu.make_async_copy(stage.at[slot, pl.ds(0, n)], out.at[b, pl.ds(r - 1, n)], out_sem.at[slot])

    ahead = n_slots - 1
    for idx in range(min(ahead, len(chunks))):
        copy_in(idx).start()
    for idx in range(len(chunks)):
        copy_in(idx).wait()
        copy_out(idx).start()
        if idx >= 1:
            copy_out(idx - 1).wait()
        if idx + ahead < len(chunks):
            copy_in(idx + ahead).start()
    copy_out(len(chunks) - 1).wait()

    for c in new_in:
        c.wait()
    for c in new_out:
        c.start()
    for c in new_out:
        c.wait()


def _shift_caches(k_buf, v_buf, k_new, v_new):
    bsz, wb = k_buf.shape[:2]
    rows = min(SHIFT_ROWS, wb)
    any_spec = pl.BlockSpec(memory_space=pl.ANY)
    return pl.pallas_call(
        _shift_kernel,
        in_specs=[any_spec] * 4,
        out_specs=[any_spec] * 2,
        out_shape=[jax.ShapeDtypeStruct(k_buf.shape, k_buf.dtype), jax.ShapeDtypeStruct(v_buf.shape, v_buf.dtype)],
        scratch_shapes=[pltpu.VMEM((SHIFT_SLOTS, rows) + k_buf.shape[2:], k_buf.dtype),
                        pltpu.VMEM((2,) + k_new.shape, k_new.dtype),
                        pltpu.SemaphoreType.DMA((SHIFT_SLOTS,)),
                        pltpu.SemaphoreType.DMA((SHIFT_SLOTS,)),
                        pltpu.SemaphoreType.DMA((2,))],
        compiler_params=pltpu.CompilerParams(vmem_limit_bytes=VMEM_LIMIT_BYTES),
        name="shift_caches",
    )(k_buf, v_buf, k_new, v_new)


def _row_tile(m, cap):
    return m if m <= cap else cap


def _even_weights(w_in, w_out, conv_w, conv_b, gate_w, gate_b, lam, rel_bias):
    d_a = conv_w.shape[1]
    gw = jnp.concatenate([gate_w[0], gate_w[1]], axis=-1).astype(BF16)
    return dict(w_in=w_in.astype(BF16), wa=w_out[:d_a].astype(BF16), wb=w_out[d_a:].astype(BF16),
                conv_w=conv_w, conv_b=conv_b[None], gw=gw, gate_b=gate_b, lam=lam[None],
                rel_bias=rel_bias, bias_s=_dil_sample_bias(rel_bias), d_a=d_a)


def _even_prompt(x, g_pre, g_post, w):
    s = x.shape[0]
    d_a = w["d_a"]
    d_b = w["wb"].shape[0]
    z = _norm_proj(x, g_pre, w["w_in"], tm=_row_tile(s, 1024), tn=1024)
    ya, h_last, conv_new = _rglru_prompt(z, w["conv_w"], w["conv_b"], w["gw"], w["gate_b"], w["lam"],
                                         d_a=d_a, tb=256)
    yb = _dil_prompt(z, w["rel_bias"], col0=2 * d_a // LANES, tile=B_DILATIONS[-1] * B_BLOCK)
    y = _out_proj(ya, yb, w["wa"], w["wb"], g_post, x, tm=_row_tile(s, 512))
    wbp = min(B_DILATIONS[-1] * B_BLOCK, s)
    k_state = z[s - wbp:, 2 * d_a + d_b:2 * d_a + 2 * d_b]
    v_state = z[s - wbp:, 2 * d_a + 2 * d_b:2 * d_a + 3 * d_b]
    return y, h_last, conv_new, k_state, v_state


def _even_sample(x, g_pre, g_post, w, h0, conv0, k_buf, v_buf):
    bsz = x.shape[0]
    d_a = w["d_a"]
    d_b = w["wb"].shape[0]
    wb = k_buf.shape[1]
    z = _norm_proj(x, g_pre, w["w_in"], tm=bsz, tn=1024)
    ya, h_new, conv_new_t = _rglru_sample(z, jnp.swapaxes(conv0, 0, 1), h0, w["conv_w"], w["conv_b"],
                                          w["gw"], w["gate_b"], w["lam"], d_a=d_a)
    bias_s, bias0 = w["bias_s"]
    yb = _dil_sample(z.reshape(bsz, -1, k_buf.shape[-1]), k_buf, v_buf, bias_s, bias0, row0=2 * d_a // d_b)
    k_new = z[:, 2 * d_a + d_b:2 * d_a + 2 * d_b].reshape(bsz, 1, *k_buf.shape[2:])
    v_new = z[:, 2 * d_a + 2 * d_b:2 * d_a + 3 * d_b].reshape(bsz, 1, *v_buf.shape[2:])
    k_out, v_out = _shift_caches(k_buf, v_buf, k_new, v_new)
    y = _out_proj(ya, yb.reshape(bsz, d_b), w["wa"], w["wb"], g_post, x, tm=bsz)
    return y, h_new, jnp.swapaxes(conv_new_t, 0, 1), k_out, v_out


def _dot_f32(a, b):
    return jnp.dot(a, b, precision=lax.Precision.HIGHEST, preferred_element_type=F32)


def _mlstm_prompt_kernel(q_ref, k_ref, v_ref, o_ref, g_ref, gates_ref, gbias_ref,
                         y_ref, c_out, n_out, m_out, c_s, n_s, m_s):
    lc = q_ref.shape[0]
    hd = q_ref.shape[1] // H_C
    kscale = hd ** -0.5

    @pl.when(pl.program_id(0) == 0)
    def _():
        c_s[...] = jnp.zeros_like(c_s)
        n_s[...] = jnp.zeros_like(n_s)
        m_s[...] = jnp.zeros_like(m_s)

    gpre = gates_ref[...] + gbias_ref[...]
    lane = lax.broadcasted_iota(jnp.int32, gpre.shape, 1)
    ti = lax.broadcasted_iota(jnp.int32, (lc, lc), 0)
    si = lax.broadcasted_iota(jnp.int32, (lc, lc), 1)
    causal = si <= ti
    bcum = _dot_f32(causal.astype(F32), _log_sigmoid(gpre))
    mix = jnp.where(lane < H_C, gpre, bcum)
    mix_t = mix.T

    for h in range(H_C):
        sl = slice(h * hd, (h + 1) * hd)
        i_col, b_col = mix[:, h:h + 1], mix[:, H_C + h:H_C + h + 1]
        i_row, b_row = mix_t[h:h + 1, :], mix_t[H_C + h:H_C + h + 1, :]
        m0 = m_s[h:h + 1, 0:1]
        q = q_ref[:, sl]
        qb = q.astype(BF16)
        ks = k_ref[:, sl] * kscale
        kb = ks.astype(BF16)
        v = v_ref[:, sl]
        c0 = c_s[h]
        n0 = n_s[h:h + 1, :]

        dm = jnp.where(causal, b_col - b_row + i_row, NEG_INF)
        inter = b_col + m0
        mt = jnp.maximum(inter, jnp.max(dm, axis=1, keepdims=True))
        wqk = jnp.exp(dm - mt) * _dot_nt(qb, kb)
        g = jnp.exp(inter - mt)
        num = _dot(wqk.astype(BF16), v.astype(BF16)) + g * _dot_nt(qb, c0.astype(BF16))
        den = jnp.sum(wqk, axis=1, keepdims=True) + g * jnp.sum(q * n0, axis=1, keepdims=True)
        hh = num / jnp.maximum(jnp.abs(den), jnp.exp(-mt))
        y_ref[:, sl] = (_sigmoid(o_ref[:, sl]) * hh * _silu(g_ref[:, sl])).astype(y_ref.dtype)

        b_last = b_col[lc - 1:lc, :]
        wlast = b_last - b_col + i_col
        m_new = jnp.maximum(b_last + m0, jnp.max(wlast, axis=0, keepdims=True))
        ws = jnp.exp(wlast - m_new)
        g_last = jnp.exp(b_last + m0 - m_new)
        c_new = g_last * c0 + _dot_tn((v * ws).astype(BF16), kb)
        n_new = g_last * n0 + jnp.sum(ws * ks, axis=0, keepdims=True)
        c_s[h] = c_new
        n_s[h:h + 1, :] = n_new
        m_s[h:h + 1, :] = jnp.broadcast_to(m_new, (1, LANES))
        c_out[h] = c_new
        n_out[h:h + 1, :] = n_new
    m_out[...] = m_s[...]


def _mlstm_prompt(z, gates, gbias, *, d_c):
    s = z.shape[0]
    lc = C_CHUNK if s % C_CHUNK == 0 else s
    hd = d_c // H_C
    full = lambda shape: pl.BlockSpec(shape, lambda c: (0,) * len(shape))
    sec = lambda j: pl.BlockSpec((lc, d_c), lambda c, j=j: (c, j))
    return pl.pallas_call(
        _mlstm_prompt_kernel,
        grid=(s // lc,),
        in_specs=[sec(0), sec(1), sec(2), sec(3), sec(4),
                  pl.BlockSpec((lc, LANES), lambda c: (c, 0)), full((1, LANES))],
        out_specs=[pl.BlockSpec((lc, d_c), lambda c: (c, 0)),
                   full((H_C, hd, hd)), full((H_C, hd)), full((8, LANES))],
        out_shape=[jax.ShapeDtypeStruct((s, d_c), BF16),
                   jax.ShapeDtypeStruct((H_C, hd, hd), F32),
                   jax.ShapeDtypeStruct((H_C, hd), F32),
                   jax.ShapeDtypeStruct((8, LANES), F32)],
        scratch_shapes=[pltpu.VMEM((H_C, hd, hd), F32), pltpu.VMEM((H_C, hd), F32),
                        pltpu.VMEM((8, LANES), F32)],
        compiler_params=_params("arbitrary"),
        name="mlstm_prompt",
    )(z, z, z, z, z, gates, gbias)


def _mlstm_sample_kernel(q_ref, k_ref, v_ref, o_ref, g_ref, gates_ref, gbias_ref, c_ref, n_ref, m_ref,
                         y_ref, c_out, n_out, m_out, lf_out):
    hd = q_ref.shape[-1] // H_C
    kscale = hd ** -0.5
    gpre = gates_ref[0] + gbias_ref[...]
    logf = _log_sigmoid(gpre)
    lf_out[0] = logf
    for h in range(H_C):
        sl = slice(h * hd, (h + 1) * hd)
        i_g = gpre[:, h:h + 1]
        f_g = logf[:, H_C + h:H_C + h + 1]
        m0 = m_ref[0, :, h:h + 1]
        q = q_ref[0, :, sl]
        ks = k_ref[0, :, sl] * kscale
        v = v_ref[0, :, sl]
        c0 = c_ref[0, h]
        n0 = n_ref[0, h:h + 1, :]

        inter = f_g + m0
        mt = jnp.maximum(inter, i_g)
        wqk = jnp.exp(i_g - mt) * jnp.sum(q * ks, axis=1, keepdims=True)
        g = jnp.exp(inter - mt)
        cq = _dot_nt(jnp.broadcast_to(q, (8, hd)).astype(BF16), c0.astype(BF16))[0:1, :]
        num = wqk * v + g * cq
        den = wqk + g * jnp.sum(n0 * q, axis=1, keepdims=True)
        hh = num / jnp.maximum(jnp.abs(den), jnp.exp(-mt))
        y_ref[0, :, sl] = (_sigmoid(o_ref[0, :, sl]) * hh * _silu(g_ref[0, :, sl])).astype(y_ref.dtype)

        m_new = jnp.maximum(f_g + m0, i_g)
        ws = jnp.exp(i_g - m_new)
        g_last = jnp.exp(f_g + m0 - m_new)
        v_col = jnp.broadcast_to(v, (LANES, hd)).T[:, 0:1]
        c_out[0, h] = g_last * c0 + ws * (v_col * ks)
        n_out[0, h:h + 1, :] = g_last * n0 + ws * ks
        m_out[0, :, h:h + 1] = m_new


def _mlstm_sample(z3, gates3, gbias, c0, n0, m0):
    bsz, n_h, hd = n0.shape
    d_c = n_h * hd
    sec = lambda j: pl.BlockSpec((1, 1, d_c), lambda b, j=j: (b, 0, j))
    return pl.pallas_call(
        _mlstm_sample_kernel,
        grid=(bsz,),
        in_specs=[sec(0), sec(1), sec(2), sec(3), sec(4),
                  pl.BlockSpec((1, 1, LANES), lambda b: (b, 0, 0)),
                  pl.BlockSpec((1, LANES), lambda b: (0, 0)),
                  pl.BlockSpec((1, n_h, hd, hd), lambda b: (b, 0, 0, 0)),
                  pl.BlockSpec((1, n_h, hd), lambda b: (b, 0, 0)),
                  pl.BlockSpec((1, 1, n_h), lambda b: (b, 0, 0))],
        out_specs=[pl.BlockSpec((1, 1, d_c), lambda b: (b, 0, 0)),
                   pl.BlockSpec((1, n_h, hd, hd), lambda b: (b, 0, 0, 0)),
                   pl.BlockSpec((1, n_h, hd), lambda b: (b, 0, 0)),
                   pl.BlockSpec((1, 1, n_h), lambda b: (b, 0, 0)),
                   pl.BlockSpec((1, 1, LANES), lambda b: (b, 0, 0))],
        out_shape=[jax.ShapeDtypeStruct((bsz, 1, d_c), BF16),
                   jax.ShapeDtypeStruct(c0.shape, F32),
                   jax.ShapeDtypeStruct(n0.shape, F32),
                   jax.ShapeDtypeStruct(m0.shape, F32),
                   jax.ShapeDtypeStruct((bsz, 1, LANES), F32)],
        compiler_params=_params("parallel"),
        name="mlstm_sample",
    )(z3, z3, z3, z3, z3, gates3, gbias, c0, n0, m0)


N_DECAY_PARTS = 3


def _fox_gates_kernel(gates_ref, gbias_ref, logf_ref, e_ref, carry_s):
    tb = gates_ref.shape[0]

    @pl.when(pl.program_id(0) == 0)
    def _():
        carry_s[...] = jnp.zeros_like(carry_s)

    logf = _log_sigmoid(gates_ref[...] + gbias_ref[...])
    logf_ref[...] = logf
    lower = (lax.broadcasted_iota(jnp.int32, (tb, tb), 1)
             <= lax.broadcasted_iota(jnp.int32, (tb, tb), 0)).astype(F32)
    cum = _dot_f32(lower, logf) + carry_s[0:1, :]
    carry_s[...] = jnp.broadcast_to(cum[tb - 1:tb, :], carry_s.shape)
    lane = lax.broadcasted_iota(jnp.int32, (tb, LANES), 1)
    for h in range(H_D):
        rest = jnp.broadcast_to(cum[:, 2 * H_C + h:2 * H_C + h + 1] * (-LOG2E), (tb, LANES))
        e = jnp.zeros((tb, LANES), F32)
        for part in range(N_DECAY_PARTS):
            piece = rest.astype(BF16).astype(F32)
            e = jnp.where(lane == part, piece, e)
            rest = rest - piece
        e_ref[h] = e.astype(BF16)


def _fox_gates(gates, gbias, *, tb):
    s = gates.shape[0]
    return pl.pallas_call(
        _fox_gates_kernel,
        grid=(s // tb,),
        in_specs=[pl.BlockSpec((tb, LANES), lambda i: (i, 0)), pl.BlockSpec((1, LANES), lambda i: (0, 0))],
        out_specs=[pl.BlockSpec((tb, LANES), lambda i: (i, 0)),
                   pl.BlockSpec((H_D, tb, LANES), lambda i: (0, i, 0))],
        out_shape=[jax.ShapeDtypeStruct((s, LANES), F32), jax.ShapeDtypeStruct((H_D, s, LANES), BF16)],
        scratch_shapes=[pltpu.VMEM((8, LANES), F32)],
        compiler_params=_params("arbitrary"),
        name="fox_gates",
    )(gates, gbias)


def _fox_prompt_kernel(q_ref, k_ref, v_ref, g_ref, e_ref, y_ref, ka_s, vt_s, m_s, l_s, acc_s, sa_s, sb_s, *, tk):
    tq, hd = q_ref.shape
    s_len = k_ref.shape[0]
    i = pl.program_id(1)
    assert tq == 2 * tk

    @pl.when(i == 0)
    def _():
        def prep(c, carry):
            st = pl.multiple_of(c * tk, tk)
            ka_s[pl.ds(st, tk), 0:hd] = k_ref[pl.ds(st, tk), :].astype(BF16)
            ka_s[pl.ds(st, tk), hd:2 * hd] = e_ref[0, pl.ds(st, tk), :]
            vt_s[:, pl.ds(st, tk)] = v_ref[pl.ds(st, tk), :].T.astype(BF16)
            return carry
        lax.fori_loop(0, s_len // tk, prep, 0)

    lane = lax.broadcasted_iota(jnp.int32, (tq, hd), 1)
    qa = jnp.concatenate([(q_ref[...] * (hd ** -0.5 * LOG2E)).astype(BF16),
                          jnp.where(lane < N_DECAY_PARTS, 1.0, 0.0).astype(BF16)], axis=1)
    m_s[...] = jnp.full_like(m_s, NEG_INF)
    l_s[...] = jnp.zeros_like(l_s)
    acc_s[...] = jnp.zeros_like(acc_s)

    def scores(blk, dst):
        st = pl.multiple_of(blk * tk, tk)
        dst[...] = _dot_nt(ka_s[pl.ds(st, tk), :], qa)

    def consume(src, blk, diag_offset):
        st = pl.multiple_of(blk * tk, tk)
        s = src[...]
        if diag_offset is not None:
            s = jnp.where(lax.broadcasted_iota(jnp.int32, s.shape, 0) + diag_offset
                          <= lax.broadcasted_iota(jnp.int32, s.shape, 1), s, NEG_INF)
        m_o = m_s[0:1, :]
        m_n = jnp.maximum(m_o, jnp.max(s, axis=0, keepdims=True))
        al = jnp.exp2(m_o - m_n)
        p = jnp.exp2(s - m_n)
        l_s[0:1, :] = l_s[0:1, :] * al + jnp.sum(p, axis=0, keepdims=True)
        acc_s[...] = acc_s[...] * al + _dot(vt_s[:, pl.ds(st, tk)], p.astype(BF16))
        m_s[0:1, :] = m_n

    scores(0, sa_s)

    def body(t, carry):
        scores(2 * t + 1, sb_s)
        consume(sa_s, 2 * t, None)
        scores(2 * t + 2, sa_s)
        consume(sb_s, 2 * t + 1, None)
        return carry

    lax.fori_loop(0, i, body, 0)
    scores(2 * i + 1, sb_s)
    consume(sa_s, 2 * i, 0)
    consume(sb_s, 2 * i + 1, tk)
    y_t = acc_s[...] / l_s[0:1, :]
    y_ref[...] = (y_t.T * _silu(g_ref[...])).astype(y_ref.dtype)


def _fox_prompt(z, e, *, col0, tq, tk):
    s = z.shape[0]
    hd = LANES
    qc, kc, vc, gc = (col0 + i * H_D for i in range(4))
    return pl.pallas_call(
        functools.partial(_fox_prompt_kernel, tk=tk),
        grid=(H_D, s // tq),
        in_specs=[pl.BlockSpec((tq, hd), lambda h, i: (i, qc + h)),
                  pl.BlockSpec((s, hd), lambda h, i: (0, kc + h)),
                  pl.BlockSpec((s, hd), lambda h, i: (0, vc + h)),
                  pl.BlockSpec((tq, hd), lambda h, i: (i, gc + h)),
                  pl.BlockSpec((1, s, hd), lambda h, i: (h, 0, 0))],
        out_specs=pl.BlockSpec((tq, hd), lambda h, i: (i, h)),
        out_shape=jax.ShapeDtypeStruct((s, H_D * hd), BF16),
        scratch_shapes=[pltpu.VMEM((s, 2 * hd), BF16), pltpu.VMEM((hd, s), BF16),
                        pltpu.VMEM((8, tq), F32), pltpu.VMEM((8, tq), F32), pltpu.VMEM((hd, tq), F32),
                        pltpu.VMEM((tk, tq), F32), pltpu.VMEM((tk, tq), F32)],
        compiler_params=_params("parallel", "arbitrary"),
        name="fox_prompt",
    )(z, z, z, z, e)


def _page_sums_kernel(lf_ref, suf_ref, tot_ref):
    x = lf_ref[...]
    width = x.shape[1]
    lane = lax.broadcasted_iota(jnp.int32, x.shape, 1)
    inc, cyc = x, x
    step = H_D
    while step < width:
        shifted = pltpu.roll(inc, width - step, axis=1)
        inc = inc + jnp.where(lane < width - step, shifted, 0.0)
        cyc = cyc + pltpu.roll(cyc, width - step, axis=1)
        step *= 2
    suf_ref[...] = inc - x
    tot_ref[...] = cyc


def _page_sums(lf_flat, *, rows):
    n_pool, width = lf_flat.shape
    spec = pl.BlockSpec((rows, width), lambda i: (i, 0))
    return pl.pallas_call(
        _page_sums_kernel,
        grid=(n_pool // rows,),
        in_specs=[spec], out_specs=[spec, spec],
        out_shape=[jax.ShapeDtypeStruct(lf_flat.shape, F32)] * 2,
        compiler_params=_params("parallel"),
        name="page_sums",
    )(lf_flat)


def _fox_sample_kernel(pt_ref, q_ref, kn_ref, vn_ref, g_ref, lfn_ref, fb_ref, *rest, pages_per_step):
    g_pages = pages_per_step
    k_refs = rest[:g_pages]
    v_refs = rest[g_pages:2 * g_pages]
    suf_refs = rest[2 * g_pages:3 * g_pages]
    tot_refs = rest[3 * g_pages:4 * g_pages]
    y_ref, m_s, l_s, acc_s, carry_s = rest[4 * g_pages:]
    j = pl.program_id(1)
    n_h, hd = q_ref.shape[1:]
    width = k_refs[0].shape[1]
    scale = hd ** -0.5
    q = q_ref[0] * scale
    qb = q.astype(BF16)
    own = (lax.broadcasted_iota(jnp.int32, (n_h, width), 1) % n_h
           == lax.broadcasted_iota(jnp.int32, (n_h, width), 0))

    @pl.when(j == 0)
    def _():
        carry_s[...] = _log_sigmoid(lfn_ref[0] + fb_ref[...])
        s_new = jnp.sum(q * kn_ref[0], axis=1, keepdims=True)
        m_s[...] = jnp.broadcast_to(s_new, m_s.shape)
        l_s[...] = jnp.ones_like(l_s)
        acc_s[...] = vn_ref[0]

    carry = carry_s[...]
    scores = []
    for i in range(g_pages):
        bias = suf_refs[i][0] + carry
        carry = carry + tot_refs[i][0]
        s = _dot_nt(qb, k_refs[i][0].astype(BF16)) + bias
        scores.append(jnp.where(own, s, NEG_INF))
    carry_s[...] = carry

    m_o = m_s[:, 0:1]
    m_n = m_o
    for s in scores:
        m_n = jnp.maximum(m_n, jnp.max(s, axis=1, keepdims=True))
    al = jnp.exp(m_o - m_n)
    l_n = l_s[:, 0:1] * al
    acc = acc_s[...] * al
    for i in range(g_pages):
        p = jnp.exp(scores[i] - m_n)
        l_n = l_n + jnp.sum(p, axis=1, keepdims=True)
        acc = acc + _dot(p.astype(BF16), v_refs[i][0].astype(BF16))
    m_s[...] = jnp.broadcast_to(m_n, m_s.shape)
    l_s[...] = jnp.broadcast_to(l_n, l_s.shape)
    acc_s[...] = acc

    @pl.when(j == pl.num_programs(1) - 1)
    def _():
        y_ref[0] = (acc / l_n * _silu(g_ref[0])).astype(y_ref.dtype)


def _fox_sample(page_table, zh, lfn_row, fbias_row, k_pool, v_pool, suf, tot, *, row0, pages_per_step):
    bsz, n_pages = page_table.shape
    _, width, hd = k_pool.shape
    n_h = H_D
    g_pages = pages_per_step
    assert n_pages % g_pages == 0
    zspec = lambda c: pl.BlockSpec((1, n_h, hd), lambda b, j, pt: (b, c, 0))

    def page_of(i):
        return lambda b, j, pt: pt[b, n_pages - 1 - (j * g_pages + i)]

    kv_specs = [pl.BlockSpec((1, width, hd), lambda b, j, pt, f=page_of(i): (f(b, j, pt), 0, 0))
                for i in range(g_pages)]
    row_specs = [pl.BlockSpec((1, 1, width), lambda b, j, pt, f=page_of(i): (f(b, j, pt), 0, 0))
                 for i in range(g_pages)]
    grid_spec = pltpu.PrefetchScalarGridSpec(
        num_scalar_prefetch=1,
        grid=(bsz, n_pages // g_pages),
        in_specs=[zspec(row0), zspec(row0 + 1), zspec(row0 + 2), zspec(row0 + 3),
                  pl.BlockSpec((1, 1, width), lambda b, j, pt: (b, 0, 0)),
                  pl.BlockSpec((1, width), lambda b, j, pt: (0, 0))]
                 + kv_specs + kv_specs + row_specs + row_specs,
        out_specs=pl.BlockSpec((1, n_h, hd), lambda b, j, pt: (b, 0, 0)),
        scratch_shapes=[pltpu.VMEM((n_h, hd), F32)] * 3 + [pltpu.VMEM((1, width), F32)],
    )
    return pl.pallas_call(
        functools.partial(_fox_sample_kernel, pages_per_step=g_pages),
        grid_spec=grid_spec,
        out_shape=jax.ShapeDtypeStruct((bsz, n_h, hd), BF16),
        compiler_params=_params("parallel", "arbitrary"),
        name="fox_sample",
    )(page_table, zh, zh, zh, zh, lfn_row, fbias_row, *([k_pool] * g_pages), *([v_pool] * g_pages),
      *([suf] * g_pages), *([tot] * g_pages))


def _odd_weights(w_in, w_out, c_gate_b, d_f_b):
    d_d = H_D * LANES
    d_c = w_out.shape[0] - d_d
    g0 = 5 * d_c
    g1 = g0 + 2 * H_C + 4 * d_d
    d = w_in.shape[0]
    w_main = jnp.concatenate([w_in[:, :g0], w_in[:, g0 + 2 * H_C:g1]], axis=1).astype(BF16)
    n_g = 2 * H_C + H_D
    w_gate = jnp.concatenate([w_in[:, g0:g0 + 2 * H_C], w_in[:, g1:g1 + H_D],
                              jnp.zeros((d, LANES - n_g), w_in.dtype)], axis=1).astype(BF16)
    gbias = jnp.concatenate([c_gate_b[0], c_gate_b[1], d_f_b, jnp.zeros((LANES - n_g,), F32)])[None]
    return dict(w_in=w_main, w_gate=w_gate, gbias=gbias, fbias=d_f_b[None, :],
                wa=w_out[:d_c].astype(BF16), wb=w_out[d_c:].astype(BF16), d_c=d_c, d_d=d_d)


def _odd_prompt(x, g_pre, g_post, w):
    s = x.shape[0]
    d_c, d_d = w["d_c"], w["d_d"]
    z, gates = _norm_proj(x, g_pre, w["w_in"], w["w_gate"], tm=_row_tile(s, 1024), tn=1024)
    yc, c_new, n_new, m8 = _mlstm_prompt(z, gates, w["gbias"], d_c=d_c)
    logf, decay = _fox_gates(gates, w["gbias"], tb=_row_tile(s, 512))
    yd = _fox_prompt(z, decay, col0=5 * d_c // LANES, tq=_row_tile(s, 1024), tk=_row_tile(s, 512))
    y = _out_proj(yc, yd, w["wa"], w["wb"], g_post, x, tm=_row_tile(s, 512))
    k_d = z[:, 5 * d_c + d_d:5 * d_c + 2 * d_d]
    v_d = z[:, 5 * d_c + 2 * d_d:5 * d_c + 3 * d_d]
    return y, c_new, n_new, m8[:H_C, 0], k_d, v_d, logf[:, 2 * H_C:2 * H_C + H_D]


def _odd_sample(x, g_pre, g_post, w, c0, n0, m0, k_pool, v_pool, lf_pool, page_table):
    bsz = x.shape[0]
    d_c, d_d = w["d_c"], w["d_d"]
    z, gates = _norm_proj(x, g_pre, w["w_in"], w["w_gate"], tm=bsz, tn=1024)
    z3 = z.reshape(bsz, 1, -1)
    yc, c_new, n_new, m_new, lf_row = _mlstm_sample(z3, gates.reshape(bsz, 1, LANES), w["gbias"],
                                                     c0, n0, m0.reshape(bsz, 1, H_C))
    n_pool, page, n_h, hd = k_pool.shape
    width = page * n_h
    suf, tot = _page_sums(lf_pool.reshape(n_pool, width), rows=math.gcd(n_pool, 256))
    lfn_row = jnp.tile(gates[:, 2 * H_C:2 * H_C + H_D], (1, page)).reshape(bsz, 1, width)
    yd = _fox_sample(page_table, z.reshape(bsz, -1, hd), lfn_row, jnp.tile(w["fbias"], (1, page)),
                     k_pool.reshape(n_pool, width, hd), v_pool.reshape(n_pool, width, hd),
                     suf.reshape(n_pool, 1, width), tot.reshape(n_pool, 1, width),
                     row0=5 * d_c // d_d, pages_per_step=8)
    y = _out_proj(yc.reshape(bsz, d_c), yd.reshape(bsz, d_d), w["wa"], w["wb"], g_post, x, tm=bsz)
    k_d = z[:, 5 * d_c + d_d:5 * d_c + 2 * d_d]
    v_d = z[:, 5 * d_c + 2 * d_d:5 * d_c + 3 * d_d]
    return y, c_new, n_new, m_new.reshape(bsz, H_C), k_d, v_d, lf_row[:, 0, 2 * H_C:2 * H_C + H_D]


def kernel(x_prompt, x_sample, state_a_h, state_a_conv, cache_b_k, cache_b_v, state_c_C, state_c_n,
           state_c_m, cache_d_k, cache_d_v, cache_d_logf, page_table, norm_pre, norm_post, w_in_even,
           w_out_even, a_conv_w, a_conv_b, a_gate_w, a_gate_b, a_lambda, rel_bias, w_in_odd, w_out_odd,
           c_gate_b, d_f_b):
    bp, s, _ = x_prompt.shape
    bs = x_sample.shape[0]
    assert x_sample.shape[1] == 1
    xp = [x_prompt[b] for b in range(bp)]
    xs = x_sample[:, 0]
    names = ("ah", "ac", "bk", "bv", "cC", "cn", "cm", "dk", "dv", "dl")
    outs_p = {n: [] for n in names}
    outs_s = {n: [] for n in names}
    hb = cache_b_k.shape[3:]
    hd = cache_d_k.shape[3:]
    for l in range(norm_pre.shape[0]):
        j = l // 2
        g_pre, g_post = norm_pre[l][None], norm_post[l][None]
        if l % 2 == 0:
            w = _even_weights(w_in_even[j], w_out_even[j], a_conv_w[j], a_conv_b[j], a_gate_w[j],
                              a_gate_b[j], a_lambda[j], rel_bias)
            res = [_even_prompt(x, g_pre, g_post, w) for x in xp]
            xp = [r[0] for r in res]
            outs_p["ah"].append(jnp.concatenate([r[1] for r in res], axis=0))
            outs_p["ac"].append(jnp.stack([r[2] for r in res]))
            outs_p["bk"].append(jnp.stack([r[3].reshape(-1, *hb) for r in res]))
            outs_p["bv"].append(jnp.stack([r[4].reshape(-1, *hb) for r in res]))
            xs, h_new, conv_new, k_out, v_out = _even_sample(
                xs, g_pre, g_post, w, state_a_h[j], state_a_conv[j], cache_b_k[j], cache_b_v[j])
            outs_s["ah"].append(h_new)
            outs_s["ac"].append(conv_new)
            outs_s["bk"].append(k_out)
            outs_s["bv"].append(v_out)
        else:
            w = _odd_weights(w_in_odd[j], w_out_odd[j], c_gate_b[j], d_f_b[j])
            res = [_odd_prompt(x, g_pre, g_post, w) for x in xp]
            xp = [r[0] for r in res]
            outs_p["cC"].append(jnp.stack([r[1] for r in res]))
            outs_p["cn"].append(jnp.stack([r[2] for r in res]))
            outs_p["cm"].append(jnp.stack([r[3] for r in res]))
            outs_p["dk"].append(jnp.stack([r[4].reshape(s, *hd) for r in res]))
            outs_p["dv"].append(jnp.stack([r[5].reshape(s, *hd) for r in res]))
            outs_p["dl"].append(jnp.stack([r[6] for r in res]))
            xs, c_new, n_new, m_new, k_d, v_d, lf_d = _odd_sample(
                xs, g_pre, g_post, w, state_c_C[j], state_c_n[j], state_c_m[j],
                cache_d_k[j], cache_d_v[j], cache_d_logf[j], page_table)
            outs_s["cC"].append(c_new)
            outs_s["cn"].append(n_new)
            outs_s["cm"].append(m_new)
            outs_s["dk"].append(k_d.reshape(bs, 1, *hd))
            outs_s["dv"].append(v_d.reshape(bs, 1, *hd))
            outs_s["dl"].append(lf_d.reshape(bs, 1, -1))
    st = jnp.stack
    return (st(xp), xs[:, None, :],
            st(outs_p["ah"]), st(outs_s["ah"]), st(outs_p["ac"]), st(outs_s["ac"]),
            st(outs_p["bk"]), st(outs_p["bv"]), st(outs_s["bk"]), st(outs_s["bv"]),
            st(outs_p["cC"]), st(outs_p["cn"]), st(outs_p["cm"]),
            st(outs_s["cC"]), st(outs_s["cn"]), st(outs_s["cm"]),
            st(outs_p["dk"]), st(outs_p["dv"]), st(outs_p["dl"]),
            st(outs_s["dk"]), st(outs_s["dv"]), st(outs_s["dl"]))
```

```python
import functools
import math

import numpy as np
import jax
import jax.numpy as jnp
from jax import lax
from jax.experimental import pallas as pl
from jax.experimental.pallas import tpu as pltpu

F32 = jnp.float32
BF16 = jnp.bfloat16

LANES = 128
VMEM_LIMIT_BYTES = 56 * 2**20

RMS_EPS = 1e-6
LRU_C = 8.0
A_BLOCKS = 8
CONV_W = 4
H_B = 8
B_BLOCK = 128
B_DILATIONS = (1, 4, 16)
N_BUCKETS = 32
REL_MAX_DIST = 2048
H_C = 4
C_CHUNK = 128
H_D = 8
PAGE_SIZE = 128
NEG_INF = float("-inf")
LOG2E = math.log2(math.e)


def _params(*semantics):
    return pltpu.CompilerParams(dimension_semantics=semantics, vmem_limit_bytes=VMEM_LIMIT_BYTES)


def _sigmoid(x):
    return 1.0 / (1.0 + jnp.exp(-x))


def _silu(x):
    return x * _sigmoid(x)


def _softplus(x):
    return jnp.maximum(x, 0.0) + jnp.log1p(jnp.exp(-jnp.abs(x)))


def _log_sigmoid(x):
    return -_softplus(-x)


def _dot(a, b):
    return jnp.dot(a, b, preferred_element_type=F32)


def _dot_nt(a, b):
    return lax.dot_general(a, b, (((1,), (1,)), ((), ())), preferred_element_type=F32)


def _dot_tn(a, b):
    return lax.dot_general(a, b, (((0,), (0,)), ((), ())), preferred_element_type=F32)


def _norm_proj_kernel(x_ref, g_ref, w_ref, *rest, with_gates):
    if with_gates:
        wg_ref, o_ref, og_ref, hn_ref = rest
    else:
        o_ref, hn_ref = rest

    @pl.when(pl.program_id(1) == 0)
    def _():
        x = x_ref[...]
        ms = jnp.mean(x * x, axis=-1, keepdims=True)
        hn = (x * lax.rsqrt(ms + RMS_EPS) * g_ref[...]).astype(BF16)
        hn_ref[...] = hn
        if with_gates:
            og_ref[...] = _dot(hn, wg_ref[...])

    o_ref[...] = _dot(hn_ref[...], w_ref[...])


def _norm_proj(x, g, w, wg=None, *, tm, tn):
    m, d = x.shape
    n = w.shape[1]
    with_gates = wg is not None
    in_specs = [pl.BlockSpec((tm, d), lambda i, j: (i, 0)),
                pl.BlockSpec((1, d), lambda i, j: (0, 0)),
                pl.BlockSpec((d, tn), lambda i, j: (0, j))]
    out_specs = [pl.BlockSpec((tm, tn), lambda i, j: (i, j))]
    out_shape = [jax.ShapeDtypeStruct((m, n), F32)]
    args = [x, g, w]
    if with_gates:
        in_specs.append(pl.BlockSpec((d, LANES), lambda i, j: (0, 0)))
        out_specs.append(pl.BlockSpec((tm, LANES), lambda i, j: (i, 0)))
        out_shape.append(jax.ShapeDtypeStruct((m, LANES), F32))
        args.append(wg)
    outs = pl.pallas_call(
        functools.partial(_norm_proj_kernel, with_gates=with_gates),
        grid=(m // tm, n // tn),
        in_specs=in_specs, out_specs=out_specs, out_shape=out_shape,
        scratch_shapes=[pltpu.VMEM((tm, d), BF16)],
        compiler_params=_params("parallel", "arbitrary"),
        name="norm_proj",
    )(*args)
    return outs if with_gates else outs[0]


def _out_proj_kernel(ya_ref, yb_ref, wa_ref, wb_ref, g_ref, x_ref, o_ref):
    y = _dot(ya_ref[...], wa_ref[...]) + _dot(yb_ref[...], wb_ref[...])
    ms = jnp.mean(y * y, axis=-1, keepdims=True)
    o_ref[...] = x_ref[...] + y * lax.rsqrt(ms + RMS_EPS) * g_ref[...]


def _out_proj(ya, yb, wa, wb, g, x, *, tm):
    m, d = x.shape
    ka, kb = ya.shape[1], yb.shape[1]
    return pl.pallas_call(
        _out_proj_kernel,
        grid=(m // tm,),
        in_specs=[pl.BlockSpec((tm, ka), lambda i: (i, 0)),
                  pl.BlockSpec((tm, kb), lambda i: (i, 0)),
                  pl.BlockSpec((ka, d), lambda i: (0, 0)),
                  pl.BlockSpec((kb, d), lambda i: (0, 0)),
                  pl.BlockSpec((1, d), lambda i: (0, 0)),
                  pl.BlockSpec((tm, d), lambda i: (i, 0))],
        out_specs=pl.BlockSpec((tm, d), lambda i: (i, 0)),
        out_shape=jax.ShapeDtypeStruct((m, d), F32),
        compiler_params=_params("parallel"),
        name="out_proj",
    )(ya, yb, wa, wb, g, x)


def _rglru_gates(xc, ga_w_ref, gate_b_ref, sp_lam):
    d_a = xc.shape[1]
    bw = d_a // A_BLOCKS
    pre_r, pre_i = [], []
    for n in range(A_BLOCKS):
        pre = _dot(xc[:, n * bw:(n + 1) * bw].astype(BF16), ga_w_ref[n])
        pre_r.append(pre[:, :bw])
        pre_i.append(pre[:, bw:])
    r = _sigmoid(jnp.concatenate(pre_r, axis=1) + gate_b_ref[0:1, :])
    i = _sigmoid(jnp.concatenate(pre_i, axis=1) + gate_b_ref[1:2, :])
    log_a = -LRU_C * r * sp_lam
    a = jnp.exp(log_a)
    b = jnp.sqrt(-jnp.tanh(log_a) * (a * a + 1.0)) * i * xc
    return a, b


def _rglru_prompt_kernel(xa_ref, ga_ref, cw_ref, cb_ref, gw_ref, gb_ref, lam_ref,
                         ya_ref, hlast_ref, conv_ref, h_s, xprev_s):
    step = pl.program_id(0)
    tb = xa_ref.shape[0]

    @pl.when(step == 0)
    def _():
        h_s[...] = jnp.zeros_like(h_s)
        xprev_s[...] = jnp.zeros_like(xprev_s)

    xa = xa_ref[...]
    xcat = jnp.concatenate([xprev_s[...], xa], axis=0)
    xc = (cw_ref[0:1, :] * xcat[5:5 + tb] + cw_ref[1:2, :] * xcat[6:6 + tb]
          + cw_ref[2:3, :] * xcat[7:7 + tb] + cw_ref[3:4, :] * xa) + cb_ref[...]
    a, b = _rglru_gates(xc, gw_ref, gb_ref, _softplus(-lam_ref[...]))

    row = lax.broadcasted_iota(jnp.int32, a.shape, 0)
    s = 1
    while s < tb:
        keep = row >= s
        a_sh = jnp.where(keep, pltpu.roll(a, s, axis=0), 1.0)
        b_sh = jnp.where(keep, pltpu.roll(b, s, axis=0), 0.0)
        b = a * b_sh + b
        a = a * a_sh
        s *= 2
    h = a * h_s[...] + b
    ya_ref[...] = (h * _silu(ga_ref[...])).astype(ya_ref.dtype)
    h_s[...] = h[tb - 1:tb, :]
    hlast_ref[...] = h[tb - 1:tb, :]
    conv_ref[...] = xa[tb - (CONV_W - 1):tb, :]
    xprev_s[...] = xa[tb - 8:tb, :]


def _rglru_prompt(z, conv_w, conv_b, gw, gate_b, lam, *, d_a, tb):
    s = z.shape[0]
    bw = d_a // A_BLOCKS
    full = lambda shape: pl.BlockSpec(shape, lambda i: (0,) * len(shape))
    return pl.pallas_call(
        _rglru_prompt_kernel,
        grid=(s // tb,),
        in_specs=[pl.BlockSpec((tb, d_a), lambda i: (i, 0)),
                  pl.BlockSpec((tb, d_a), lambda i: (i, 1)),
                  full((CONV_W, d_a)), full((1, d_a)), full((A_BLOCKS, bw, 2 * bw)),
                  full((2, d_a)), full((1, d_a))],
        out_specs=[pl.BlockSpec((tb, d_a), lambda i: (i, 0)),
                   full((1, d_a)), full((CONV_W - 1, d_a))],
        out_shape=[jax.ShapeDtypeStruct((s, d_a), BF16),
                   jax.ShapeDtypeStruct((1, d_a), F32),
                   jax.ShapeDtypeStruct((CONV_W - 1, d_a), F32)],
        scratch_shapes=[pltpu.VMEM((1, d_a), F32), pltpu.VMEM((8, d_a), F32)],
        compiler_params=_params("arbitrary"),
        name="rglru_prompt",
    )(z, z, conv_w, conv_b, gw, gate_b, lam)


def _rglru_sample_kernel(xa_ref, ga_ref, c0_ref, h0_ref, cw_ref, cb_ref, gw_ref, gb_ref, lam_ref,
                         ya_ref, h_ref, conv_ref):
    xa = xa_ref[...]
    xc = (cw_ref[0:1, :] * c0_ref[0] + cw_ref[1:2, :] * c0_ref[1]
          + cw_ref[2:3, :] * c0_ref[2] + cw_ref[3:4, :] * xa) + cb_ref[...]
    a, b = _rglru_gates(xc, gw_ref, gb_ref, _softplus(-lam_ref[...]))
    h = a * h0_ref[...] + b
    ya_ref[...] = (h * _silu(ga_ref[...])).astype(ya_ref.dtype)
    h_ref[...] = h
    conv_ref[0] = c0_ref[1]
    conv_ref[1] = c0_ref[2]
    conv_ref[2] = xa


def _rglru_sample(z, conv0_t, h0, conv_w, conv_b, gw, gate_b, lam, *, d_a):
    bsz = z.shape[0]
    bw = d_a // A_BLOCKS
    full = lambda shape: pl.BlockSpec(shape, lambda i: (0,) * len(shape))
    return pl.pallas_call(
        _rglru_sample_kernel,
        grid=(1,),
        in_specs=[pl.BlockSpec((bsz, d_a), lambda i: (0, 0)),
                  pl.BlockSpec((bsz, d_a), lambda i: (0, 1)),
                  full((CONV_W - 1, bsz, d_a)), full((bsz, d_a)),
                  full((CONV_W, d_a)), full((1, d_a)), full((A_BLOCKS, bw, 2 * bw)),
                  full((2, d_a)), full((1, d_a))],
        out_specs=[full((bsz, d_a)), full((bsz, d_a)), full((CONV_W - 1, bsz, d_a))],
        out_shape=[jax.ShapeDtypeStruct((bsz, d_a), BF16),
                   jax.ShapeDtypeStruct((bsz, d_a), F32),
                   jax.ShapeDtypeStruct((CONV_W - 1, bsz, d_a), F32)],
        compiler_params=_params("arbitrary"),
        name="rglru_sample",
    )(z, z, conv0_t, h0, conv_w, conv_b, gw, gate_b, lam)


def _t5_bucket(dist):
    exact = N_BUCKETS // 2
    n = np.maximum(dist, 1).astype(np.float32)
    large = exact + (np.log(n / np.float32(exact)) / np.float32(math.log(REL_MAX_DIST / exact))
                     * np.float32(N_BUCKETS - exact)).astype(np.int32)
    return np.where(dist < exact, dist, np.minimum(large, N_BUCKETS - 1))


def _dil_prompt_buckets():
    qi = np.arange(B_BLOCK)[:, None]
    ki = np.arange(2 * B_BLOCK)[None, :]
    rel = qi + B_BLOCK - ki
    valid = (rel >= 0) & (rel <= B_BLOCK)
    return np.stack([np.where(valid, _t5_bucket(np.maximum(rel, 0) * dil), -1)
                     for dil in B_DILATIONS]).astype(np.int32)


DIL_GROUP = 8


def _dil_prompt_kernel(q_ref, kc_ref, kp_ref, vc_ref, vp_ref, gb_ref, bucket_ref, rb_ref, o_ref,
                       bias_ref, *scratch):
    n_pat = len(B_DILATIONS)
    o_s = scratch[:n_pat]
    lse_s = scratch[n_pat:]
    tile, hd = q_ref.shape
    scale = hd ** -0.5 * LOG2E
    head = pl.program_id(0)
    not_first = pl.program_id(1) > 0

    @pl.when(pl.program_id(1) == 0)
    def _():
        for p_idx in range(n_pat):
            bucket = bucket_ref[p_idx]
            bias = jnp.full(bucket.shape, NEG_INF, F32)
            for b in range(N_BUCKETS):
                bias = jnp.where(bucket == b, rb_ref[b, head] * LOG2E, bias)
            bias_ref[p_idx] = bias

    def rows(start, n, d):
        return pl.ds(start, n, stride=d) if d > 1 else pl.ds(start, n)

    cur_half = lax.broadcasted_iota(jnp.int32, (1, B_BLOCK, 2 * B_BLOCK), 2) >= B_BLOCK

    for p_idx, d in enumerate(B_DILATIONS):
        unit = d * B_BLOCK
        blocks = [(u, c) for u in range(tile // unit) for c in range(d)]
        for g0 in range(0, len(blocks), DIL_GROUP):
            group = blocks[g0:g0 + DIL_GROUP]
            q_rows, qs, ks, vs = [], [], [], []
            for u, c in group:
                r = rows(u * unit + c, B_BLOCK, d)
                q_rows.append(r)
                qs.append(q_ref[r, :])
                if u == 0:
                    pr = rows(tile - unit + c, B_BLOCK, d)
                    cr = rows(c, B_BLOCK, d)
                    ks.append(jnp.concatenate([kp_ref[pr, :], kc_ref[cr, :]], axis=0))
                    vs.append(jnp.concatenate([vp_ref[pr, :], vc_ref[cr, :]], axis=0))
                else:
                    r2 = rows((u - 1) * unit + c, 2 * B_BLOCK, d)
                    ks.append(kc_ref[r2, :])
                    vs.append(vc_ref[r2, :])
            q3 = (jnp.stack(qs) * scale).astype(BF16)
            k3 = jnp.stack(ks).astype(BF16)
            v3 = jnp.stack(vs).astype(BF16)
            s3 = jnp.einsum("bqd,bkd->bqk", q3, k3, preferred_element_type=F32) + bias_ref[p_idx][None]
            n_first = sum(1 for u, _ in group if u == 0)
            if n_first:
                masked = jnp.where(jnp.logical_or(not_first, cur_half), s3[:n_first], NEG_INF)
                s3 = masked if n_first == len(group) else jnp.concatenate([masked, s3[n_first:]], axis=0)
            m3 = jnp.max(s3, axis=-1, keepdims=True)
            p3 = jnp.exp2(s3 - m3)
            l3 = jnp.sum(p3, axis=-1, keepdims=True)
            o3 = jnp.einsum("bqk,bkd->bqd", p3.astype(BF16), v3, preferred_element_type=F32) / l3
            lse3 = jnp.broadcast_to(m3 + jnp.log2(l3), o3.shape)
            for i, r in enumerate(q_rows):
                o_s[p_idx][r, :] = o3[i]
                lse_s[p_idx][r, :] = lse3[i]

    lses = [ref[...] for ref in lse_s]
    m = functools.reduce(jnp.maximum, lses)
    ws = [jnp.exp2(x - m) for x in lses]
    num = functools.reduce(lambda a, b: a + b, [w * ref[...] for w, ref in zip(ws, o_s)])
    den = functools.reduce(lambda a, b: a + b, ws)
    o_ref[...] = (num / den * _silu(gb_ref[...])).astype(o_ref.dtype)


def _dil_prompt(z, rel_bias, *, col0, tile):
    s = z.shape[0]
    hd = LANES
    n_pat = len(B_DILATIONS)
    qc, kc, vc, gc = (col0 + i * H_B for i in range(4))
    prev = lambda t: jnp.maximum(t - 1, 0)
    blk = lambda shape, fn: pl.BlockSpec(shape, fn)
    return pl.pallas_call(
        _dil_prompt_kernel,
        grid=(H_B, s // tile),
        in_specs=[blk((tile, hd), lambda h, t: (t, qc + h)),
                  blk((tile, hd), lambda h, t: (t, kc + h)),
                  blk((tile, hd), lambda h, t: (prev(t), kc + h)),
                  blk((tile, hd), lambda h, t: (t, vc + h)),
                  blk((tile, hd), lambda h, t: (prev(t), vc + h)),
                  blk((tile, hd), lambda h, t: (t, gc + h)),
                  blk((n_pat, B_BLOCK, 2 * B_BLOCK), lambda h, t: (0, 0, 0)),
                  pl.BlockSpec(memory_space=pltpu.SMEM)],
        out_specs=blk((tile, hd), lambda h, t: (t, h)),
        out_shape=jax.ShapeDtypeStruct((s, H_B * hd), BF16),
        scratch_shapes=[pltpu.VMEM((n_pat, B_BLOCK, 2 * B_BLOCK), F32)]
                       + [pltpu.VMEM((tile, hd), F32)] * (2 * n_pat),
        compiler_params=_params("parallel", "arbitrary"),
        name="dilated_prompt",
    )(z, z, z, z, z, z, jnp.asarray(_dil_prompt_buckets()), rel_bias)


def _dil_sample_kernel(q_ref, kn_ref, vn_ref, gb_ref, *rest):
    n_pat = len(B_DILATIONS)
    k_refs = rest[:n_pat]
    v_refs = rest[n_pat:2 * n_pat]
    bias_ref, bias0_ref, o_ref = rest[2 * n_pat:]
    n_h, hd = q_ref.shape[1:]
    q = q_ref[0] * hd ** -0.5
    qb = q.astype(BF16)
    s_new = jnp.sum(q * kn_ref[0], axis=1, keepdims=True) + bias0_ref[...]
    s = [_dot_nt(qb, k_refs[g][0].reshape(B_BLOCK * n_h, hd).astype(BF16)) + bias_ref[g]
         for g in range(n_pat)]
    m = s_new
    for sg in s:
        m = jnp.maximum(m, jnp.max(sg, axis=1, keepdims=True))
    p_new = n_pat * jnp.exp(s_new - m)
    den = p_new
    num = p_new * vn_ref[0]
    for g in range(n_pat):
        p = jnp.exp(s[g] - m)
        den = den + jnp.sum(p, axis=1, keepdims=True)
        num = num + _dot(p.astype(BF16), v_refs[g][0].reshape(B_BLOCK * n_h, hd).astype(BF16))
    o_ref[0] = (num / den * _silu(gb_ref[0])).astype(o_ref.dtype)


def _dil_sample_bias(rel_bias):
    n_h = rel_bias.shape[1]
    j = np.arange(B_BLOCK)
    own = (np.arange(B_BLOCK * n_h)[None, :] % n_h) == np.arange(n_h)[:, None]
    tabs = []
    for dil in B_DILATIONS:
        flat = rel_bias[_t5_bucket((B_BLOCK - j) * dil)].reshape(1, B_BLOCK * n_h)
        tabs.append(jnp.where(own, flat, NEG_INF))
    return jnp.stack(tabs), rel_bias[_t5_bucket(np.zeros((1,), np.int64))].T


def _dil_sample(zh, k_buf, v_buf, bias, bias0, *, row0):
    bsz, wb, n_h, hd = k_buf.shape
    views, specs = [], []
    for buf in (k_buf, v_buf):
        for d in B_DILATIONS:
            assert wb % (d * B_BLOCK) == 0
            views.append(buf.reshape(bsz, wb // d, d * n_h, hd))
            last = wb // d // B_BLOCK - 1
            specs.append(pl.BlockSpec((1, B_BLOCK, n_h, hd), lambda b, last=last: (b, last, 0, 0)))
    zspec = lambda c: pl.BlockSpec((1, n_h, hd), lambda b: (b, c, 0))
    n_pat = len(B_DILATIONS)
    return pl.pallas_call(
        _dil_sample_kernel,
        grid=(bsz,),
        in_specs=[zspec(row0), zspec(row0 + 1), zspec(row0 + 2), zspec(row0 + 3)] + specs
                 + [pl.BlockSpec((n_pat, n_h, B_BLOCK * n_h), lambda b: (0, 0, 0)),
                    pl.BlockSpec((n_h, 1), lambda b: (0, 0))],
        out_specs=pl.BlockSpec((1, n_h, hd), lambda b: (b, 0, 0)),
        out_shape=jax.ShapeDtypeStruct((bsz, n_h, hd), BF16),
        compiler_params=_params("parallel"),
        name="dilated_sample",
    )(zh, zh, zh, zh, *views, bias, bias0)


SHIFT_SLOTS = 4
SHIFT_ROWS = 1024


def _shift_kernel(kb_ref, vb_ref, kn_ref, vn_ref, ko_ref, vo_ref, stage, new_stage, in_sem, out_sem, new_sem):
    bsz, wb = kb_ref.shape[0], kb_ref.shape[1]
    n_slots, rows = stage.shape[0], stage.shape[1]

    new_in = [pltpu.make_async_copy(new, new_stage.at[i], new_sem.at[i])
              for i, new in enumerate((kn_ref, vn_ref))]
    new_out = [pltpu.make_async_copy(new_stage.at[i], out.at[:, pl.ds(wb - 1, 1)], new_sem.at[i])
               for i, out in enumerate((ko_ref, vo_ref))]
    for c in new_in:
        c.start()

    chunks = []
    for buf, out in ((kb_ref, ko_ref), (vb_ref, vo_ref)):
        for b in range(bsz):
            r = 1
            while r < wb:
                n = min(rows, wb - r)
                chunks.append((buf, out, b, r, n))
                r += n

    def copy_in(idx):
        buf, _, b, r, n = chunks[idx]
        slot = idx % n_slots
        return pltpu.make_async_copy(buf.at[b, pl.ds(r, n)], stage.at[slot, pl.ds(0, n)], in_sem.at[slot])

    def copy_out(idx):
        _, out, b, r, n = chunks[idx]
        slot = idx % n_slots
        return pltpu.make_async_copy(stage.at[slot, pl.ds(0, n)], out.at[b, pl.ds(r - 1, n)], out_sem.at[slot])

    ahead = n_slots - 1
    for idx in range(min(ahead, len(chunks))):
        copy_in(idx).start()
    for idx in range(len(chunks)):
        copy_in(idx).wait()
        copy_out(idx).start()
        if idx >= 1:
            copy_out(idx - 1).wait()
        if idx + ahead < len(chunks):
            copy_in(idx + ahead).start()
    copy_out(len(chunks) - 1).wait()

    for c in new_in:
        c.wait()
    for c in new_out:
        c.start()
    for c in new_out:
        c.wait()


def _shift_caches(k_buf, v_buf, k_new, v_new):
    bsz, wb = k_buf.shape[:2]
    rows = min(SHIFT_ROWS, wb)
    any_spec = pl.BlockSpec(memory_space=pl.ANY)
    return pl.pallas_call(
        _shift_kernel,
        in_specs=[any_spec] * 4,
        out_specs=[any_spec] * 2,
        out_shape=[jax.ShapeDtypeStruct(k_buf.shape, k_buf.dtype), jax.ShapeDtypeStruct(v_buf.shape, v_buf.dtype)],
        scratch_shapes=[pltpu.VMEM((SHIFT_SLOTS, rows) + k_buf.shape[2:], k_buf.dtype),
                        pltpu.VMEM((2,) + k_new.shape, k_new.dtype),
                        pltpu.SemaphoreType.DMA((SHIFT_SLOTS,)),
                        pltpu.SemaphoreType.DMA((SHIFT_SLOTS,)),
                        pltpu.SemaphoreType.DMA((2,))],
        compiler_params=pltpu.CompilerParams(vmem_limit_bytes=VMEM_LIMIT_BYTES),
        name="shift_caches",
    )(k_buf, v_buf, k_new, v_new)


def _row_tile(m, cap):
    return m if m <= cap else cap


def _even_weights(w_in, w_out, conv_w, conv_b, gate_w, gate_b, lam, rel_bias):
    d_a = conv_w.shape[1]
    gw = jnp.concatenate([gate_w[0], gate_w[1]], axis=-1).astype(BF16)
    return dict(w_in=w_in.astype(BF16), wa=w_out[:d_a].astype(BF16), wb=w_out[d_a:].astype(BF16),
                conv_w=conv_w, conv_b=conv_b[None], gw=gw, gate_b=gate_b, lam=lam[None],
                rel_bias=rel_bias, bias_s=_dil_sample_bias(rel_bias), d_a=d_a)


def _even_prompt(x, g_pre, g_post, w):
    s = x.shape[0]
    d_a = w["d_a"]
    d_b = w["wb"].shape[0]
    z = _norm_proj(x, g_pre, w["w_in"], tm=_row_tile(s, 1024), tn=1024)
    ya, h_last, conv_new = _rglru_prompt(z, w["conv_w"], w["conv_b"], w["gw"], w["gate_b"], w["lam"],
                                         d_a=d_a, tb=256)
    yb = _dil_prompt(z, w["rel_bias"], col0=2 * d_a // LANES, tile=B_DILATIONS[-1] * B_BLOCK)
    y = _out_proj(ya, yb, w["wa"], w["wb"], g_post, x, tm=_row_tile(s, 512))
    wbp = min(B_DILATIONS[-1] * B_BLOCK, s)
    k_state = z[s - wbp:, 2 * d_a + d_b:2 * d_a + 2 * d_b]
    v_state = z[s - wbp:, 2 * d_a + 2 * d_b:2 * d_a + 3 * d_b]
    return y, h_last, conv_new, k_state, v_state


def _even_sample(x, g_pre, g_post, w, h0, conv0, k_buf, v_buf, *, shift_now):
    bsz = x.shape[0]
    d_a = w["d_a"]
    d_b = w["wb"].shape[0]
    z = _norm_proj(x, g_pre, w["w_in"], tm=bsz, tn=1024)
    ya, h_new, conv_new_t = _rglru_sample(z, jnp.swapaxes(conv0, 0, 1), h0, w["conv_w"], w["conv_b"],
                                          w["gw"], w["gate_b"], w["lam"], d_a=d_a)
    bias_s, bias0 = w["bias_s"]
    yb = _dil_sample(z.reshape(bsz, -1, k_buf.shape[-1]), k_buf, v_buf, bias_s, bias0, row0=2 * d_a // d_b)
    k_new = z[:, 2 * d_a + d_b:2 * d_a + 2 * d_b].reshape(bsz, 1, *k_buf.shape[2:])
    v_new = z[:, 2 * d_a + 2 * d_b:2 * d_a + 3 * d_b].reshape(bsz, 1, *v_buf.shape[2:])
    k_out, v_out = _shift_caches(k_buf, v_buf, k_new, v_new) if shift_now else (k_new, v_new)
    y = _out_proj(ya, yb.reshape(bsz, d_b), w["wa"], w["wb"], g_post, x, tm=bsz)
    return y, h_new, jnp.swapaxes(conv_new_t, 0, 1), k_out, v_out


def _dot_f32(a, b):
    return jnp.dot(a, b, precision=lax.Precision.HIGHEST, preferred_element_type=F32)


def _mlstm_prompt_kernel(q_ref, k_ref, v_ref, o_ref, g_ref, gates_ref, gbias_ref,
                         y_ref, c_out, n_out, m_out, c_s, n_s, m_s):
    lc = q_ref.shape[0]
    hd = q_ref.shape[1] // H_C
    kscale = hd ** -0.5

    @pl.when(pl.program_id(0) == 0)
    def _():
        c_s[...] = jnp.zeros_like(c_s)
        n_s[...] = jnp.zeros_like(n_s)
        m_s[...] = jnp.zeros_like(m_s)

    gpre = gates_ref[...] + gbias_ref[...]
    lane = lax.broadcasted_iota(jnp.int32, gpre.shape, 1)
    ti = lax.broadcasted_iota(jnp.int32, (lc, lc), 0)
    si = lax.broadcasted_iota(jnp.int32, (lc, lc), 1)
    causal = si <= ti
    bcum = _dot_f32(causal.astype(F32), _log_sigmoid(gpre))
    mix = jnp.where(lane < H_C, gpre, bcum)
    mix_t = mix.T

    for h in range(H_C):
        sl = slice(h * hd, (h + 1) * hd)
        i_col, b_col = mix[:, h:h + 1], mix[:, H_C + h:H_C + h + 1]
        i_row, b_row = mix_t[h:h + 1, :], mix_t[H_C + h:H_C + h + 1, :]
        m0 = m_s[h:h + 1, 0:1]
        q = q_ref[:, sl]
        qb = q.astype(BF16)
        ks = k_ref[:, sl] * kscale
        kb = ks.astype(BF16)
        v = v_ref[:, sl]
        c0 = c_s[h]
        n0 = n_s[h:h + 1, :]

        dm = jnp.where(causal, b_col - b_row + i_row, NEG_INF)
        inter = b_col + m0
        mt = jnp.maximum(inter, jnp.max(dm, axis=1, keepdims=True))
        wqk = jnp.exp(dm - mt) * _dot_nt(qb, kb)
        g = jnp.exp(inter - mt)
        num = _dot(wqk.astype(BF16), v.astype(BF16)) + g * _dot_nt(qb, c0.astype(BF16))
        den = jnp.sum(wqk, axis=1, keepdims=True) + g * jnp.sum(q * n0, axis=1, keepdims=True)
        hh = num / jnp.maximum(jnp.abs(den), jnp.exp(-mt))
        y_ref[:, sl] = (_sigmoid(o_ref[:, sl]) * hh * _silu(g_ref[:, sl])).astype(y_ref.dtype)

        b_last = b_col[lc - 1:lc, :]
        wlast = b_last - b_col + i_col
        m_new = jnp.maximum(b_last + m0, jnp.max(wlast, axis=0, keepdims=True))
        ws = jnp.exp(wlast - m_new)
        g_last = jnp.exp(b_last + m0 - m_new)
        c_new = g_last * c0 + _dot_tn((v * ws).astype(BF16), kb)
        n_new = g_last * n0 + jnp.sum(ws * ks, axis=0, keepdims=True)
        c_s[h] = c_new
        n_s[h:h + 1, :] = n_new
        m_s[h:h + 1, :] = jnp.broadcast_to(m_new, (1, LANES))
        c_out[h] = c_new
        n_out[h:h + 1, :] = n_new
    m_out[...] = m_s[...]


def _mlstm_prompt(z, gates, gbias, *, d_c):
    s = z.shape[0]
    lc = C_CHUNK if s % C_CHUNK == 0 else s
    hd = d_c // H_C
    full = lambda shape: pl.BlockSpec(shape, lambda c: (0,) * len(shape))
    sec = lambda j: pl.BlockSpec((lc, d_c), lambda c, j=j: (c, j))
    return pl.pallas_call(
        _mlstm_prompt_kernel,
        grid=(s // lc,),
        in_specs=[sec(0), sec(1), sec(2), sec(3), sec(4),
                  pl.BlockSpec((lc, LANES), lambda c: (c, 0)), full((1, LANES))],
        out_specs=[pl.BlockSpec((lc, d_c), lambda c: (c, 0)),
                   full((H_C, hd, hd)), full((H_C, hd)), full((8, LANES))],
        out_shape=[jax.ShapeDtypeStruct((s, d_c), BF16),
                   jax.ShapeDtypeStruct((H_C, hd, hd), F32),
                   jax.ShapeDtypeStruct((H_C, hd), F32),
                   jax.ShapeDtypeStruct((8, LANES), F32)],
        scratch_shapes=[pltpu.VMEM((H_C, hd, hd), F32), pltpu.VMEM((H_C, hd), F32),
                        pltpu.VMEM((8, LANES), F32)],
        compiler_params=_params("arbitrary"),
        name="mlstm_prompt",
    )(z, z, z, z, z, gates, gbias)


def _mlstm_sample_kernel(q_ref, k_ref, v_ref, o_ref, g_ref, gates_ref, gbias_ref, c_ref, n_ref, m_ref,
                         y_ref, c_out, n_out, m_out, lf_out):
    hd = q_ref.shape[-1] // H_C
    kscale = hd ** -0.5
    gpre = gates_ref[0] + gbias_ref[...]
    logf = _log_sigmoid(gpre)
    lf_out[0] = logf
    for h in range(H_C):
        sl = slice(h * hd, (h + 1) * hd)
        i_g = gpre[:, h:h + 1]
        f_g = logf[:, H_C + h:H_C + h + 1]
        m0 = m_ref[0, :, h:h + 1]
        q = q_ref[0, :, sl]
        ks = k_ref[0, :, sl] * kscale
        v = v_ref[0, :, sl]
        c0 = c_ref[0, h]
        n0 = n_ref[0, h:h + 1, :]

        inter = f_g + m0
        mt = jnp.maximum(inter, i_g)
        wqk = jnp.exp(i_g - mt) * jnp.sum(q * ks, axis=1, keepdims=True)
        g = jnp.exp(inter - mt)
        cq = _dot_nt(jnp.broadcast_to(q, (8, hd)).astype(BF16), c0.astype(BF16))[0:1, :]
        num = wqk * v + g * cq
        den = wqk + g * jnp.sum(n0 * q, axis=1, keepdims=True)
        hh = num / jnp.maximum(jnp.abs(den), jnp.exp(-mt))
        y_ref[0, :, sl] = (_sigmoid(o_ref[0, :, sl]) * hh * _silu(g_ref[0, :, sl])).astype(y_ref.dtype)

        m_new = jnp.maximum(f_g + m0, i_g)
        ws = jnp.exp(i_g - m_new)
        g_last = jnp.exp(f_g + m0 - m_new)
        v_col = jnp.broadcast_to(v, (LANES, hd)).T[:, 0:1]
        c_out[0, h] = g_last * c0 + ws * (v_col * ks)
        n_out[0, h:h + 1, :] = g_last * n0 + ws * ks
        m_out[0, :, h:h + 1] = m_new


def _mlstm_sample(z3, gates3, gbias, c0, n0, m0):
    bsz, n_h, hd = n0.shape
    d_c = n_h * hd
    sec = lambda j: pl.BlockSpec((1, 1, d_c), lambda b, j=j: (b, 0, j))
    return pl.pallas_call(
        _mlstm_sample_kernel,
        grid=(bsz,),
        in_specs=[sec(0), sec(1), sec(2), sec(3), sec(4),
                  pl.BlockSpec((1, 1, LANES), lambda b: (b, 0, 0)),
                  pl.BlockSpec((1, LANES), lambda b: (0, 0)),
                  pl.BlockSpec((1, n_h, hd, hd), lambda b: (b, 0, 0, 0)),
                  pl.BlockSpec((1, n_h, hd), lambda b: (b, 0, 0)),
                  pl.BlockSpec((1, 1, n_h), lambda b: (b, 0, 0))],
        out_specs=[pl.BlockSpec((1, 1, d_c), lambda b: (b, 0, 0)),
                   pl.BlockSpec((1, n_h, hd, hd), lambda b: (b, 0, 0, 0)),
                   pl.BlockSpec((1, n_h, hd), lambda b: (b, 0, 0)),
                   pl.BlockSpec((1, 1, n_h), lambda b: (b, 0, 0)),
                   pl.BlockSpec((1, 1, LANES), lambda b: (b, 0, 0))],
        out_shape=[jax.ShapeDtypeStruct((bsz, 1, d_c), BF16),
                   jax.ShapeDtypeStruct(c0.shape, F32),
                   jax.ShapeDtypeStruct(n0.shape, F32),
                   jax.ShapeDtypeStruct(m0.shape, F32),
                   jax.ShapeDtypeStruct((bsz, 1, LANES), F32)],
        compiler_params=_params("parallel"),
        name="mlstm_sample",
    )(z3, z3, z3, z3, z3, gates3, gbias, c0, n0, m0)


N_DECAY_PARTS = 3


def _fox_gates_kernel(gates_ref, gbias_ref, logf_ref, e_ref, carry_s):
    tb = gates_ref.shape[0]

    @pl.when(pl.program_id(0) == 0)
    def _():
        carry_s[...] = jnp.zeros_like(carry_s)

    logf = _log_sigmoid(gates_ref[...] + gbias_ref[...])
    logf_ref[...] = logf
    lower = (lax.broadcasted_iota(jnp.int32, (tb, tb), 1)
             <= lax.broadcasted_iota(jnp.int32, (tb, tb), 0)).astype(F32)
    cum = _dot_f32(lower, logf) + carry_s[0:1, :]
    carry_s[...] = jnp.broadcast_to(cum[tb - 1:tb, :], carry_s.shape)
    lane = lax.broadcasted_iota(jnp.int32, (tb, LANES), 1)
    for h in range(H_D):
        rest = jnp.broadcast_to(cum[:, 2 * H_C + h:2 * H_C + h + 1] * (-LOG2E), (tb, LANES))
        e = jnp.zeros((tb, LANES), F32)
        for part in range(N_DECAY_PARTS):
            piece = rest.astype(BF16).astype(F32)
            e = jnp.where(lane == part, piece, e)
            rest = rest - piece
        e_ref[h] = e.astype(BF16)


def _fox_gates(gates, gbias, *, tb):
    s = gates.shape[0]
    return pl.pallas_call(
        _fox_gates_kernel,
        grid=(s // tb,),
        in_specs=[pl.BlockSpec((tb, LANES), lambda i: (i, 0)), pl.BlockSpec((1, LANES), lambda i: (0, 0))],
        out_specs=[pl.BlockSpec((tb, LANES), lambda i: (i, 0)),
                   pl.BlockSpec((H_D, tb, LANES), lambda i: (0, i, 0))],
        out_shape=[jax.ShapeDtypeStruct((s, LANES), F32), jax.ShapeDtypeStruct((H_D, s, LANES), BF16)],
        scratch_shapes=[pltpu.VMEM((8, LANES), F32)],
        compiler_params=_params("arbitrary"),
        name="fox_gates",
    )(gates, gbias)


BG_PARTS = 4


def _bg_parts(wb):
    rows = -(-(wb - 1) // BG_PARTS)
    parts, r = [], 1
    while r < wb:
        n = min(rows, wb - r)
        parts.append((r, n))
        r += n
    return parts, rows


class _BackgroundShift:
    def __init__(self, t, n_steps, refs, scratch):
        self.t, self.n_steps = t, n_steps
        self.kb, self.vb, self.kn, self.vn, self.ko, self.vo = refs
        self.stage, self.new_stage, self.in_sem, self.out_sem, self.new_sem = scratch
        self.bsz, self.wb = self.kb.shape[:2]
        self.parts, _ = _bg_parts(self.wb)
        self.n_pairs = 2 * self.bsz
        self.per_step = -(-self.n_pairs // n_steps)

    def _slot(self, step, u):
        return ((step % 2) * self.per_step + u) * len(self.parts)

    def _in(self, buf, b, p, slot):
        r0, n = self.parts[p]
        return pltpu.make_async_copy(buf.at[b, pl.ds(r0, n)], self.stage.at[slot + p, pl.ds(0, n)],
                                     self.in_sem.at[slot + p])

    def _out(self, out, b, p, slot):
        r0, n = self.parts[p]
        return pltpu.make_async_copy(self.stage.at[slot + p, pl.ds(0, n)], out.at[b, pl.ds(r0 - 1, n)],
                                     self.out_sem.at[slot + p])

    def _pair(self, step, u, cond, fn):
        pair = step * self.per_step + u
        slot = self._slot(step, u)

        @pl.when(jnp.logical_and(cond, pair < self.bsz))
        def _():
            fn(self.kb, self.ko, pair, slot)

        @pl.when(jnp.logical_and(cond, jnp.logical_and(pair >= self.bsz, pair < self.n_pairs)))
        def _():
            fn(self.vb, self.vo, pair - self.bsz, slot)

    def _new_in(self):
        return [pltpu.make_async_copy(new, self.new_stage.at[i], self.new_sem.at[i])
                for i, new in enumerate((self.kn, self.vn))]

    def _new_out(self):
        return [pltpu.make_async_copy(self.new_stage.at[i], out.at[:, pl.ds(self.wb - 1, 1)], self.new_sem.at[i])
                for i, out in enumerate((self.ko, self.vo))]

    def begin_step(self):
        t = self.t
        n_parts = range(len(self.parts))

        @pl.when(t == 0)
        def _():
            for c in self._new_in():
                c.start()

        def reads_to_writes(buf, out, b, slot):
            for p in n_parts:
                self._in(buf, b, p, slot).wait()
                self._out(out, b, p, slot).start()

        def wait_writes(buf, out, b, slot):
            for p in n_parts:
                self._out(out, b, p, slot).wait()

        def start_reads(buf, out, b, slot):
            for p in n_parts:
                self._in(buf, b, p, slot).start()

        for u in range(self.per_step):
            self._pair(t - 1, u, t >= 1, reads_to_writes)
            self._pair(t - 2, u, t >= 2, wait_writes)
            self._pair(t, u, t >= 0, start_reads)

    def end_step(self):
        t = self.t
        last = t == self.n_steps - 1
        n_parts = range(len(self.parts))

        def finish(buf, out, b, slot):
            for p in n_parts:
                self._in(buf, b, p, slot).wait()
                self._out(out, b, p, slot).start()
            for p in n_parts:
                self._out(out, b, p, slot).wait()

        def wait_writes(buf, out, b, slot):
            for p in n_parts:
                self._out(out, b, p, slot).wait()

        for u in range(self.per_step):
            self._pair(t - 1, u, jnp.logical_and(last, t >= 1), wait_writes)
            self._pair(t, u, last, finish)

        @pl.when(last)
        def _():
            for c in self._new_in():
                c.wait()
            for c in self._new_out():
                c.start()
            for c in self._new_out():
                c.wait()


def _fox_prompt_kernel(q_ref, k_ref, v_ref, g_ref, e_ref, *rest, tk, grid, with_shift):
    if with_shift:
        shift_in, rest = rest[:4], rest[4:]
        y_ref, shift_out, rest = rest[0], rest[1:3], rest[3:]
        ka_s, vt_s, m_s, l_s, acc_s, sa_s, sb_s = rest[:7]
        shift = _BackgroundShift(pl.program_id(0) * grid[1] + pl.program_id(1), grid[0] * grid[1],
                                 tuple(shift_in) + tuple(shift_out), rest[7:])
        shift.begin_step()
    else:
        y_ref, ka_s, vt_s, m_s, l_s, acc_s, sa_s, sb_s = rest
    tq, hd = q_ref.shape
    s_len = k_ref.shape[0]
    i = pl.program_id(1)
    assert tq == 2 * tk

    @pl.when(i == 0)
    def _():
        def prep(c, carry):
            st = pl.multiple_of(c * tk, tk)
            ka_s[pl.ds(st, tk), 0:hd] = k_ref[pl.ds(st, tk), :].astype(BF16)
            ka_s[pl.ds(st, tk), hd:2 * hd] = e_ref[0, pl.ds(st, tk), :]
            vt_s[:, pl.ds(st, tk)] = v_ref[pl.ds(st, tk), :].T.astype(BF16)
            return carry
        lax.fori_loop(0, s_len // tk, prep, 0)

    lane = lax.broadcasted_iota(jnp.int32, (tq, hd), 1)
    qa = jnp.concatenate([(q_ref[...] * (hd ** -0.5 * LOG2E)).astype(BF16),
                          jnp.where(lane < N_DECAY_PARTS, 1.0, 0.0).astype(BF16)], axis=1)
    m_s[...] = jnp.full_like(m_s, NEG_INF)
    l_s[...] = jnp.zeros_like(l_s)
    acc_s[...] = jnp.zeros_like(acc_s)

    def scores(blk, dst):
        st = pl.multiple_of(blk * tk, tk)
        dst[...] = _dot_nt(ka_s[pl.ds(st, tk), :], qa)

    def consume(src, blk, diag_offset):
        st = pl.multiple_of(blk * tk, tk)
        s = src[...]
        if diag_offset is not None:
            s = jnp.where(lax.broadcasted_iota(jnp.int32, s.shape, 0) + diag_offset
                          <= lax.broadcasted_iota(jnp.int32, s.shape, 1), s, NEG_INF)
        m_o = m_s[0:1, :]
        m_n = jnp.maximum(m_o, jnp.max(s, axis=0, keepdims=True))
        al = jnp.exp2(m_o - m_n)
        p = jnp.exp2(s - m_n)
        l_s[0:1, :] = l_s[0:1, :] * al + jnp.sum(p, axis=0, keepdims=True)
        acc_s[...] = acc_s[...] * al + _dot(vt_s[:, pl.ds(st, tk)], p.astype(BF16))
        m_s[0:1, :] = m_n

    scores(0, sa_s)

    def body(t, carry):
        scores(2 * t + 1, sb_s)
        consume(sa_s, 2 * t, None)
        scores(2 * t + 2, sa_s)
        consume(sb_s, 2 * t + 1, None)
        return carry

    lax.fori_loop(0, i, body, 0)
    scores(2 * i + 1, sb_s)
    consume(sa_s, 2 * i, 0)
    consume(sb_s, 2 * i + 1, tk)
    y_t = acc_s[...] / l_s[0:1, :]
    y_ref[...] = (y_t.T * _silu(g_ref[...])).astype(y_ref.dtype)
    if with_shift:
        shift.end_step()


def _fox_prompt(z, e, shift=None, *, col0, tq, tk):
    s = z.shape[0]
    hd = LANES
    qc, kc, vc, gc = (col0 + i * H_D for i in range(4))
    grid = (H_D, s // tq)
    once = dict(pipeline_mode=pl.Buffered(1))
    in_specs = [pl.BlockSpec((tq, hd), lambda h, i: (i, qc + h)),
                pl.BlockSpec((s, hd), lambda h, i: (0, kc + h), **once),
                pl.BlockSpec((s, hd), lambda h, i: (0, vc + h), **once),
                pl.BlockSpec((tq, hd), lambda h, i: (i, gc + h)),
                pl.BlockSpec((1, s, hd), lambda h, i: (h, 0, 0), **once)]
    out_specs = [pl.BlockSpec((tq, hd), lambda h, i: (i, h))]
    out_shape = [jax.ShapeDtypeStruct((s, H_D * hd), BF16)]
    scratch = [pltpu.VMEM((s, 2 * hd), BF16), pltpu.VMEM((hd, s), BF16),
               pltpu.VMEM((8, tq), F32), pltpu.VMEM((8, tq), F32), pltpu.VMEM((hd, tq), F32),
               pltpu.VMEM((tk, tq), F32), pltpu.VMEM((tk, tq), F32)]
    args = [z, z, z, z, e]
    if shift is not None:
        k_buf, v_buf, k_new, v_new = shift
        bsz, wb = k_buf.shape[:2]
        parts, rows = _bg_parts(wb)
        per_step = -(-2 * bsz // (grid[0] * grid[1]))
        n_slots = 2 * per_step * len(parts)
        any_spec = pl.BlockSpec(memory_space=pl.ANY)
        in_specs += [any_spec] * 4
        out_specs += [any_spec] * 2
        out_shape += [jax.ShapeDtypeStruct(k_buf.shape, k_buf.dtype), jax.ShapeDtypeStruct(v_buf.shape, v_buf.dtype)]
        scratch += [pltpu.VMEM((n_slots, rows) + k_buf.shape[2:], k_buf.dtype),
                    pltpu.VMEM((2,) + k_new.shape, k_new.dtype),
                    pltpu.SemaphoreType.DMA((n_slots,)), pltpu.SemaphoreType.DMA((n_slots,)),
                    pltpu.SemaphoreType.DMA((2,))]
        args += [k_buf, v_buf, k_new, v_new]
    outs = pl.pallas_call(
        functools.partial(_fox_prompt_kernel, tk=tk, grid=grid, with_shift=shift is not None),
        grid=grid,
        in_specs=in_specs, out_specs=out_specs, out_shape=out_shape,
        scratch_shapes=scratch,
        compiler_params=_params("arbitrary", "arbitrary"),
        name="fox_prompt",
    )(*args)
    return outs if shift is not None else outs[0]


def _page_sums_kernel(lf_ref, suf_ref, tot_ref):
    x = lf_ref[...]
    width = x.shape[1]
    lane = lax.broadcasted_iota(jnp.int32, x.shape, 1)
    inc, cyc = x, x
    step = H_D
    while step < width:
        shifted = pltpu.roll(inc, width - step, axis=1)
        inc = inc + jnp.where(lane < width - step, shifted, 0.0)
        cyc = cyc + pltpu.roll(cyc, width - step, axis=1)
        step *= 2
    suf_ref[...] = inc - x
    tot_ref[...] = cyc


def _page_sums(lf_flat, *, rows):
    n_pool, width = lf_flat.shape
    spec = pl.BlockSpec((rows, width), lambda i: (i, 0))
    return pl.pallas_call(
        _page_sums_kernel,
        grid=(n_pool // rows,),
        in_specs=[spec], out_specs=[spec, spec],
        out_shape=[jax.ShapeDtypeStruct(lf_flat.shape, F32)] * 2,
        compiler_params=_params("parallel"),
        name="page_sums",
    )(lf_flat)


def _fox_sample_kernel(pt_ref, q_ref, kn_ref, vn_ref, g_ref, lfn_ref, fb_ref, *rest, pages_per_step):
    g_pages = pages_per_step
    k_refs = rest[:g_pages]
    v_refs = rest[g_pages:2 * g_pages]
    suf_refs = rest[2 * g_pages:3 * g_pages]
    tot_refs = rest[3 * g_pages:4 * g_pages]
    y_ref, m_s, l_s, acc_s, carry_s = rest[4 * g_pages:]
    j = pl.program_id(1)
    n_h, hd = q_ref.shape[1:]
    width = k_refs[0].shape[1]
    scale = hd ** -0.5
    q = q_ref[0] * scale
    qb = q.astype(BF16)
    own = (lax.broadcasted_iota(jnp.int32, (n_h, width), 1) % n_h
           == lax.broadcasted_iota(jnp.int32, (n_h, width), 0))

    @pl.when(j == 0)
    def _():
        carry_s[...] = _log_sigmoid(lfn_ref[0] + fb_ref[...])
        s_new = jnp.sum(q * kn_ref[0], axis=1, keepdims=True)
        m_s[...] = jnp.broadcast_to(s_new, m_s.shape)
        l_s[...] = jnp.ones_like(l_s)
        acc_s[...] = vn_ref[0]

    carry = carry_s[...]
    scores = []
    for i in range(g_pages):
        bias = suf_refs[i][0] + carry
        carry = carry + tot_refs[i][0]
        s = _dot_nt(qb, k_refs[i][0].astype(BF16)) + bias
        scores.append(jnp.where(own, s, NEG_INF))
    carry_s[...] = carry

    m_o = m_s[:, 0:1]
    m_n = m_o
    for s in scores:
        m_n = jnp.maximum(m_n, jnp.max(s, axis=1, keepdims=True))
    al = jnp.exp(m_o - m_n)
    l_n = l_s[:, 0:1] * al
    acc = acc_s[...] * al
    for i in range(g_pages):
        p = jnp.exp(scores[i] - m_n)
        l_n = l_n + jnp.sum(p, axis=1, keepdims=True)
        acc = acc + _dot(p.astype(BF16), v_refs[i][0].astype(BF16))
    m_s[...] = jnp.broadcast_to(m_n, m_s.shape)
    l_s[...] = jnp.broadcast_to(l_n, l_s.shape)
    acc_s[...] = acc

    @pl.when(j == pl.num_programs(1) - 1)
    def _():
        y_ref[0] = (acc / l_n * _silu(g_ref[0])).astype(y_ref.dtype)


def _fox_sample(page_table, zh, lfn_row, fbias_row, k_pool, v_pool, suf, tot, *, row0, pages_per_step):
    bsz, n_pages = page_table.shape
    _, width, hd = k_pool.shape
    n_h = H_D
    g_pages = pages_per_step
    assert n_pages % g_pages == 0
    zspec = lambda c: pl.BlockSpec((1, n_h, hd), lambda b, j, pt: (b, c, 0))

    def page_of(i):
        return lambda b, j, pt: pt[b, n_pages - 1 - (j * g_pages + i)]

    kv_specs = [pl.BlockSpec((1, width, hd), lambda b, j, pt, f=page_of(i): (f(b, j, pt), 0, 0))
                for i in range(g_pages)]
    row_specs = [pl.BlockSpec((1, 1, width), lambda b, j, pt, f=page_of(i): (f(b, j, pt), 0, 0))
                 for i in range(g_pages)]
    grid_spec = pltpu.PrefetchScalarGridSpec(
        num_scalar_prefetch=1,
        grid=(bsz, n_pages // g_pages),
        in_specs=[zspec(row0), zspec(row0 + 1), zspec(row0 + 2), zspec(row0 + 3),
                  pl.BlockSpec((1, 1, width), lambda b, j, pt: (b, 0, 0)),
                  pl.BlockSpec((1, width), lambda b, j, pt: (0, 0))]
                 + kv_specs + kv_specs + row_specs + row_specs,
        out_specs=pl.BlockSpec((1, n_h, hd), lambda b, j, pt: (b, 0, 0)),
        scratch_shapes=[pltpu.VMEM((n_h, hd), F32)] * 3 + [pltpu.VMEM((1, width), F32)],
    )
    return pl.pallas_call(
        functools.partial(_fox_sample_kernel, pages_per_step=g_pages),
        grid_spec=grid_spec,
        out_shape=jax.ShapeDtypeStruct((bsz, n_h, hd), BF16),
        compiler_params=_params("parallel", "arbitrary"),
        name="fox_sample",
    )(page_table, zh, zh, zh, zh, lfn_row, fbias_row, *([k_pool] * g_pages), *([v_pool] * g_pages),
      *([suf] * g_pages), *([tot] * g_pages))


def _odd_weights(w_in, w_out, c_gate_b, d_f_b):
    d_d = H_D * LANES
    d_c = w_out.shape[0] - d_d
    g0 = 5 * d_c
    g1 = g0 + 2 * H_C + 4 * d_d
    d = w_in.shape[0]
    w_main = jnp.concatenate([w_in[:, :g0], w_in[:, g0 + 2 * H_C:g1]], axis=1).astype(BF16)
    n_g = 2 * H_C + H_D
    w_gate = jnp.concatenate([w_in[:, g0:g0 + 2 * H_C], w_in[:, g1:g1 + H_D],
                              jnp.zeros((d, LANES - n_g), w_in.dtype)], axis=1).astype(BF16)
    gbias = jnp.concatenate([c_gate_b[0], c_gate_b[1], d_f_b, jnp.zeros((LANES - n_g,), F32)])[None]
    return dict(w_in=w_main, w_gate=w_gate, gbias=gbias, fbias=d_f_b[None, :],
                wa=w_out[:d_c].astype(BF16), wb=w_out[d_c:].astype(BF16), d_c=d_c, d_d=d_d)


def _odd_prompt(x, g_pre, g_post, w, shift=None):
    s = x.shape[0]
    d_c, d_d = w["d_c"], w["d_d"]
    z, gates = _norm_proj(x, g_pre, w["w_in"], w["w_gate"], tm=_row_tile(s, 1024), tn=1024)
    yc, c_new, n_new, m8 = _mlstm_prompt(z, gates, w["gbias"], d_c=d_c)
    logf, decay = _fox_gates(gates, w["gbias"], tb=_row_tile(s, 512))
    yd = _fox_prompt(z, decay, shift, col0=5 * d_c // LANES, tq=_row_tile(s, 1024), tk=_row_tile(s, 512))
    shifted = None
    if shift is not None:
        yd, *shifted = yd
    y = _out_proj(yc, yd, w["wa"], w["wb"], g_post, x, tm=_row_tile(s, 512))
    k_d = z[:, 5 * d_c + d_d:5 * d_c + 2 * d_d]
    v_d = z[:, 5 * d_c + 2 * d_d:5 * d_c + 3 * d_d]
    return y, c_new, n_new, m8[:H_C, 0], k_d, v_d, logf[:, 2 * H_C:2 * H_C + H_D], shifted


def _odd_sample(x, g_pre, g_post, w, c0, n0, m0, k_pool, v_pool, lf_pool, page_table):
    bsz = x.shape[0]
    d_c, d_d = w["d_c"], w["d_d"]
    z, gates = _norm_proj(x, g_pre, w["w_in"], w["w_gate"], tm=bsz, tn=1024)
    z3 = z.reshape(bsz, 1, -1)
    yc, c_new, n_new, m_new, lf_row = _mlstm_sample(z3, gates.reshape(bsz, 1, LANES), w["gbias"],
                                                     c0, n0, m0.reshape(bsz, 1, H_C))
    n_pool, page, n_h, hd = k_pool.shape
    width = page * n_h
    suf, tot = _page_sums(lf_pool.reshape(n_pool, width), rows=math.gcd(n_pool, 256))
    lfn_row = jnp.tile(gates[:, 2 * H_C:2 * H_C + H_D], (1, page)).reshape(bsz, 1, width)
    yd = _fox_sample(page_table, z.reshape(bsz, -1, hd), lfn_row, jnp.tile(w["fbias"], (1, page)),
                     k_pool.reshape(n_pool, width, hd), v_pool.reshape(n_pool, width, hd),
                     suf.reshape(n_pool, 1, width), tot.reshape(n_pool, 1, width),
                     row0=5 * d_c // d_d, pages_per_step=8)
    y = _out_proj(yc.reshape(bsz, d_c), yd.reshape(bsz, d_d), w["wa"], w["wb"], g_post, x, tm=bsz)
    k_d = z[:, 5 * d_c + d_d:5 * d_c + 2 * d_d]
    v_d = z[:, 5 * d_c + 2 * d_d:5 * d_c + 3 * d_d]
    return y, c_new, n_new, m_new.reshape(bsz, H_C), k_d, v_d, lf_row[:, 0, 2 * H_C:2 * H_C + H_D]


def kernel(x_prompt, x_sample, state_a_h, state_a_conv, cache_b_k, cache_b_v, state_c_C, state_c_n,
           state_c_m, cache_d_k, cache_d_v, cache_d_logf, page_table, norm_pre, norm_post, w_in_even,
           w_out_even, a_conv_w, a_conv_b, a_gate_w, a_gate_b, a_lambda, rel_bias, w_in_odd, w_out_odd,
           c_gate_b, d_f_b):
    bp, s, _ = x_prompt.shape
    bs = x_sample.shape[0]
    assert x_sample.shape[1] == 1
    xp = [x_prompt[b] for b in range(bp)]
    xs = x_sample[:, 0]
    names = ("ah", "ac", "bk", "bv", "cC", "cn", "cm", "dk", "dv", "dl")
    outs_p = {n: [] for n in names}
    outs_s = {n: [] for n in names}
    hb = cache_b_k.shape[3:]
    hd = cache_d_k.shape[3:]
    depth = norm_pre.shape[0]
    pending_shift = None
    for l in range(depth):
        j = l // 2
        g_pre, g_post = norm_pre[l][None], norm_post[l][None]
        if l % 2 == 0:
            w = _even_weights(w_in_even[j], w_out_even[j], a_conv_w[j], a_conv_b[j], a_gate_w[j],
                              a_gate_b[j], a_lambda[j], rel_bias)
            res = [_even_prompt(x, g_pre, g_post, w) for x in xp]
            xp = [r[0] for r in res]
            outs_p["ah"].append(jnp.concatenate([r[1] for r in res], axis=0))
            outs_p["ac"].append(jnp.stack([r[2] for r in res]))
            outs_p["bk"].append(jnp.stack([r[3].reshape(-1, *hb) for r in res]))
            outs_p["bv"].append(jnp.stack([r[4].reshape(-1, *hb) for r in res]))
            defer = l + 1 < depth
            xs, h_new, conv_new, k_out, v_out = _even_sample(
                xs, g_pre, g_post, w, state_a_h[j], state_a_conv[j], cache_b_k[j], cache_b_v[j],
                shift_now=not defer)
            outs_s["ah"].append(h_new)
            outs_s["ac"].append(conv_new)
            if defer:
                pending_shift = (cache_b_k[j], cache_b_v[j], k_out, v_out)
            else:
                outs_s["bk"].append(k_out)
                outs_s["bv"].append(v_out)
        else:
            w = _odd_weights(w_in_odd[j], w_out_odd[j], c_gate_b[j], d_f_b[j])
            res = [_odd_prompt(x, g_pre, g_post, w, pending_shift if b == 0 else None)
                   for b, x in enumerate(xp)]
            if pending_shift is not None:
                k_out, v_out = res[0][7]
                outs_s["bk"].append(k_out)
                outs_s["bv"].append(v_out)
                pending_shift = None
            xp = [r[0] for r in res]
            outs_p["cC"].append(jnp.stack([r[1] for r in res]))
            outs_p["cn"].append(jnp.stack([r[2] for r in res]))
            outs_p["cm"].append(jnp.stack([r[3] for r in res]))
            outs_p["dk"].append(jnp.stack([r[4].reshape(s, *hd) for r in res]))
            outs_p["dv"].append(jnp.stack([r[5].reshape(s, *hd) for r in res]))
            outs_p["dl"].append(jnp.stack([r[6] for r in res]))
            xs, c_new, n_new, m_new, k_d, v_d, lf_d = _odd_sample(
                xs, g_pre, g_post, w, state_c_C[j], state_c_n[j], state_c_m[j],
                cache_d_k[j], cache_d_v[j], cache_d_logf[j], page_table)
            outs_s["cC"].append(c_new)
            outs_s["cn"].append(n_new)
            outs_s["cm"].append(m_new)
            outs_s["dk"].append(k_d.reshape(bs, 1, *hd))
            outs_s["dv"].append(v_d.reshape(bs, 1, *hd))
            outs_s["dl"].append(lf_d.reshape(bs, 1, -1))
    st = jnp.stack
    return (st(xp), xs[:, None, :],
            st(outs_p["ah"]), st(outs_s["ah"]), st(outs_p["ac"]), st(outs_s["ac"]),
            st(outs_p["bk"]), st(outs_p["bv"]), st(outs_s["bk"]), st(outs_s["bv"]),
            st(outs_p["cC"]), st(outs_p["cn"]), st(outs_p["cm"]),
            st(outs_s["cC"]), st(outs_s["cn"]), st(outs_s["cm"]),
            st(outs_p["dk"]), st(outs_p["dv"]), st(outs_p["dl"]),
            st(outs_s["dk"]), st(outs_s["dv"]), st(outs_s["dl"]))
```

```python
import functools
import math

import numpy as np
import jax
import jax.numpy as jnp
from jax import lax
from jax.experimental import pallas as pl
from jax.experimental.pallas import tpu as pltpu

F32 = jnp.float32
BF16 = jnp.bfloat16

LANES = 128
VMEM_LIMIT_BYTES = 56 * 2**20

RMS_EPS = 1e-6
LRU_C = 8.0
A_BLOCKS = 8
CONV_W = 4
H_B = 8
B_BLOCK = 128
B_DILATIONS = (1, 4, 16)
N_BUCKETS = 32
REL_MAX_DIST = 2048
H_C = 4
C_CHUNK = 128
H_D = 8
PAGE_SIZE = 128
NEG_INF = float("-inf")
LOG2E = math.log2(math.e)


def _params(*semantics):
    return pltpu.CompilerParams(dimension_semantics=semantics, vmem_limit_bytes=VMEM_LIMIT_BYTES)


def _sigmoid(x):
    return 1.0 / (1.0 + jnp.exp(-x))


def _silu(x):
    return x * _sigmoid(x)


def _softplus(x):
    return jnp.maximum(x, 0.0) + jnp.log1p(jnp.exp(-jnp.abs(x)))


def _log_sigmoid(x):
    return -_softplus(-x)


def _dot(a, b):
    return jnp.dot(a, b, preferred_element_type=F32)


def _dot_nt(a, b):
    return lax.dot_general(a, b, (((1,), (1,)), ((), ())), preferred_element_type=F32)


def _dot_tn(a, b):
    return lax.dot_general(a, b, (((0,), (0,)), ((), ())), preferred_element_type=F32)


def _norm_proj_kernel(x_ref, g_ref, w_ref, *rest, with_gates):
    if with_gates:
        wg_ref, o_ref, og_ref, hn_ref = rest
    else:
        o_ref, hn_ref = rest

    @pl.when(pl.program_id(1) == 0)
    def _():
        x = x_ref[...]
        ms = jnp.mean(x * x, axis=-1, keepdims=True)
        hn = (x * lax.rsqrt(ms + RMS_EPS) * g_ref[...]).astype(BF16)
        hn_ref[...] = hn
        if with_gates:
            og_ref[...] = _dot(hn, wg_ref[...])

    o_ref[...] = _dot(hn_ref[...], w_ref[...])


def _norm_proj(x, g, w, wg=None, *, tm, tn):
    m, d = x.shape
    n = w.shape[1]
    with_gates = wg is not None
    in_specs = [pl.BlockSpec((tm, d), lambda i, j: (i, 0)),
                pl.BlockSpec((1, d), lambda i, j: (0, 0)),
                pl.BlockSpec((d, tn), lambda i, j: (0, j))]
    out_specs = [pl.BlockSpec((tm, tn), lambda i, j: (i, j))]
    out_shape = [jax.ShapeDtypeStruct((m, n), F32)]
    args = [x, g, w]
    if with_gates:
        in_specs.append(pl.BlockSpec((d, LANES), lambda i, j: (0, 0)))
        out_specs.append(pl.BlockSpec((tm, LANES), lambda i, j: (i, 0)))
        out_shape.append(jax.ShapeDtypeStruct((m, LANES), F32))
        args.append(wg)
    outs = pl.pallas_call(
        functools.partial(_norm_proj_kernel, with_gates=with_gates),
        grid=(m // tm, n // tn),
        in_specs=in_specs, out_specs=out_specs, out_shape=out_shape,
        scratch_shapes=[pltpu.VMEM((tm, d), BF16)],
        compiler_params=_params("parallel", "arbitrary"),
        name="norm_proj",
    )(*args)
    return outs if with_gates else outs[0]


def _out_proj_kernel(ya_ref, yb_ref, wa_ref, wb_ref, g_ref, x_ref, o_ref):
    y = _dot(ya_ref[...], wa_ref[...]) + _dot(yb_ref[...], wb_ref[...])
    ms = jnp.mean(y * y, axis=-1, keepdims=True)
    o_ref[...] = x_ref[...] + y * lax.rsqrt(ms + RMS_EPS) * g_ref[...]


def _out_proj(ya, yb, wa, wb, g, x, *, tm):
    m, d = x.shape
    ka, kb = ya.shape[1], yb.shape[1]
    return pl.pallas_call(
        _out_proj_kernel,
        grid=(m // tm,),
        in_specs=[pl.BlockSpec((tm, ka), lambda i: (i, 0)),
                  pl.BlockSpec((tm, kb), lambda i: (i, 0)),
                  pl.BlockSpec((ka, d), lambda i: (0, 0)),
                  pl.BlockSpec((kb, d), lambda i: (0, 0)),
                  pl.BlockSpec((1, d), lambda i: (0, 0)),
                  pl.BlockSpec((tm, d), lambda i: (i, 0))],
        out_specs=pl.BlockSpec((tm, d), lambda i: (i, 0)),
        out_shape=jax.ShapeDtypeStruct((m, d), F32),
        compiler_params=_params("parallel"),
        name="out_proj",
    )(ya, yb, wa, wb, g, x)


def _rglru_gates(xc, ga_w_ref, gate_b_ref, sp_lam):
    d_a = xc.shape[1]
    bw = d_a // A_BLOCKS
    pre_r, pre_i = [], []
    for n in range(A_BLOCKS):
        pre = _dot(xc[:, n * bw:(n + 1) * bw].astype(BF16), ga_w_ref[n])
        pre_r.append(pre[:, :bw])
        pre_i.append(pre[:, bw:])
    r = _sigmoid(jnp.concatenate(pre_r, axis=1) + gate_b_ref[0:1, :])
    i = _sigmoid(jnp.concatenate(pre_i, axis=1) + gate_b_ref[1:2, :])
    log_a = -LRU_C * r * sp_lam
    a = jnp.exp(log_a)
    b = jnp.sqrt(-jnp.tanh(log_a) * (a * a + 1.0)) * i * xc
    return a, b


def _rglru_prompt_kernel(xa_ref, ga_ref, cw_ref, cb_ref, gw_ref, gb_ref, lam_ref,
                         ya_ref, hlast_ref, conv_ref, h_s, xprev_s):
    step = pl.program_id(0)
    tb = xa_ref.shape[0]

    @pl.when(step == 0)
    def _():
        h_s[...] = jnp.zeros_like(h_s)
        xprev_s[...] = jnp.zeros_like(xprev_s)

    xa = xa_ref[...]
    xcat = jnp.concatenate([xprev_s[...], xa], axis=0)
    xc = (cw_ref[0:1, :] * xcat[5:5 + tb] + cw_ref[1:2, :] * xcat[6:6 + tb]
          + cw_ref[2:3, :] * xcat[7:7 + tb] + cw_ref[3:4, :] * xa) + cb_ref[...]
    a, b = _rglru_gates(xc, gw_ref, gb_ref, _softplus(-lam_ref[...]))

    row = lax.broadcasted_iota(jnp.int32, a.shape, 0)
    s = 1
    while s < tb:
        keep = row >= s
        a_sh = jnp.where(keep, pltpu.roll(a, s, axis=0), 1.0)
        b_sh = jnp.where(keep, pltpu.roll(b, s, axis=0), 0.0)
        b = a * b_sh + b
        a = a * a_sh
        s *= 2
    h = a * h_s[...] + b
    ya_ref[...] = (h * _silu(ga_ref[...])).astype(ya_ref.dtype)
    h_s[...] = h[tb - 1:tb, :]
    hlast_ref[...] = h[tb - 1:tb, :]
    conv_ref[...] = xa[tb - (CONV_W - 1):tb, :]
    xprev_s[...] = xa[tb - 8:tb, :]


def _rglru_prompt(z, conv_w, conv_b, gw, gate_b, lam, *, d_a, tb):
    s = z.shape[0]
    bw = d_a // A_BLOCKS
    full = lambda shape: pl.BlockSpec(shape, lambda i: (0,) * len(shape))
    return pl.pallas_call(
        _rglru_prompt_kernel,
        grid=(s // tb,),
        in_specs=[pl.BlockSpec((tb, d_a), lambda i: (i, 0)),
                  pl.BlockSpec((tb, d_a), lambda i: (i, 1)),
                  full((CONV_W, d_a)), full((1, d_a)), full((A_BLOCKS, bw, 2 * bw)),
                  full((2, d_a)), full((1, d_a))],
        out_specs=[pl.BlockSpec((tb, d_a), lambda i: (i, 0)),
                   full((1, d_a)), full((CONV_W - 1, d_a))],
        out_shape=[jax.ShapeDtypeStruct((s, d_a), BF16),
                   jax.ShapeDtypeStruct((1, d_a), F32),
                   jax.ShapeDtypeStruct((CONV_W - 1, d_a), F32)],
        scratch_shapes=[pltpu.VMEM((1, d_a), F32), pltpu.VMEM((8, d_a), F32)],
        compiler_params=_params("arbitrary"),
        name="rglru_prompt",
    )(z, z, conv_w, conv_b, gw, gate_b, lam)


def _rglru_sample_kernel(xa_ref, ga_ref, c0_ref, h0_ref, cw_ref, cb_ref, gw_ref, gb_ref, lam_ref,
                         ya_ref, h_ref, conv_ref):
    xa = xa_ref[...]
    xc = (cw_ref[0:1, :] * c0_ref[0] + cw_ref[1:2, :] * c0_ref[1]
          + cw_ref[2:3, :] * c0_ref[2] + cw_ref[3:4, :] * xa) + cb_ref[...]
    a, b = _rglru_gates(xc, gw_ref, gb_ref, _softplus(-lam_ref[...]))
    h = a * h0_ref[...] + b
    ya_ref[...] = (h * _silu(ga_ref[...])).astype(ya_ref.dtype)
    h_ref[...] = h
    conv_ref[0] = c0_ref[1]
    conv_ref[1] = c0_ref[2]
    conv_ref[2] = xa


def _rglru_sample(z, conv0_t, h0, conv_w, conv_b, gw, gate_b, lam, *, d_a):
    bsz = z.shape[0]
    bw = d_a // A_BLOCKS
    full = lambda shape: pl.BlockSpec(shape, lambda i: (0,) * len(shape))
    return pl.pallas_call(
        _rglru_sample_kernel,
        grid=(1,),
        in_specs=[pl.BlockSpec((bsz, d_a), lambda i: (0, 0)),
                  pl.BlockSpec((bsz, d_a), lambda i: (0, 1)),
                  full((CONV_W - 1, bsz, d_a)), full((bsz, d_a)),
                  full((CONV_W, d_a)), full((1, d_a)), full((A_BLOCKS, bw, 2 * bw)),
                  full((2, d_a)), full((1, d_a))],
        out_specs=[full((bsz, d_a)), full((bsz, d_a)), full((CONV_W - 1, bsz, d_a))],
        out_shape=[jax.ShapeDtypeStruct((bsz, d_a), BF16),
                   jax.ShapeDtypeStruct((bsz, d_a), F32),
                   jax.ShapeDtypeStruct((CONV_W - 1, bsz, d_a), F32)],
        compiler_params=_params("arbitrary"),
        name="rglru_sample",
    )(z, z, conv0_t, h0, conv_w, conv_b, gw, gate_b, lam)


def _t5_bucket(dist):
    exact = N_BUCKETS // 2
    n = np.maximum(dist, 1).astype(np.float32)
    large = exact + (np.log(n / np.float32(exact)) / np.float32(math.log(REL_MAX_DIST / exact))
                     * np.float32(N_BUCKETS - exact)).astype(np.int32)
    return np.where(dist < exact, dist, np.minimum(large, N_BUCKETS - 1))


def _dil_prompt_buckets():
    qi = np.arange(B_BLOCK)[:, None]
    ki = np.arange(2 * B_BLOCK)[None, :]
    rel = qi + B_BLOCK - ki
    valid = (rel >= 0) & (rel <= B_BLOCK)
    return np.stack([np.where(valid, _t5_bucket(np.maximum(rel, 0) * dil), -1)
                     for dil in B_DILATIONS]).astype(np.int32)


DIL_GROUP = 8


def _dil_prompt_kernel(q_ref, kc_ref, kp_ref, vc_ref, vp_ref, gb_ref, bucket_ref, rb_ref, o_ref,
                       bias_ref, *scratch):
    n_pat = len(B_DILATIONS)
    o_s = scratch[:n_pat]
    lse_s = scratch[n_pat:]
    tile, hd = q_ref.shape
    scale = hd ** -0.5 * LOG2E
    head = pl.program_id(0)
    not_first = pl.program_id(1) > 0

    @pl.when(pl.program_id(1) == 0)
    def _():
        for p_idx in range(n_pat):
            bucket = bucket_ref[p_idx]
            bias = jnp.full(bucket.shape, NEG_INF, F32)
            for b in range(N_BUCKETS):
                bias = jnp.where(bucket == b, rb_ref[b, head] * LOG2E, bias)
            bias_ref[p_idx] = bias

    def rows(start, n, d):
        return pl.ds(start, n, stride=d) if d > 1 else pl.ds(start, n)

    cur_half = lax.broadcasted_iota(jnp.int32, (1, B_BLOCK, 2 * B_BLOCK), 2) >= B_BLOCK

    for p_idx, d in enumerate(B_DILATIONS):
        unit = d * B_BLOCK
        blocks = [(u, c) for u in range(tile // unit) for c in range(d)]
        for g0 in range(0, len(blocks), DIL_GROUP):
            group = blocks[g0:g0 + DIL_GROUP]
            q_rows, qs, ks, vs = [], [], [], []
            for u, c in group:
                r = rows(u * unit + c, B_BLOCK, d)
                q_rows.append(r)
                qs.append(q_ref[r, :])
                if u == 0:
                    pr = rows(tile - unit + c, B_BLOCK, d)
                    cr = rows(c, B_BLOCK, d)
                    ks.append(jnp.concatenate([kp_ref[pr, :], kc_ref[cr, :]], axis=0))
                    vs.append(jnp.concatenate([vp_ref[pr, :], vc_ref[cr, :]], axis=0))
                else:
                    r2 = rows((u - 1) * unit + c, 2 * B_BLOCK, d)
                    ks.append(kc_ref[r2, :])
                    vs.append(vc_ref[r2, :])
            q3 = (jnp.stack(qs) * scale).astype(BF16)
            k3 = jnp.stack(ks).astype(BF16)
            v3 = jnp.stack(vs).astype(BF16)
            s3 = jnp.einsum("bqd,bkd->bqk", q3, k3, preferred_element_type=F32) + bias_ref[p_idx][None]
            n_first = sum(1 for u, _ in group if u == 0)
            if n_first:
                masked = jnp.where(jnp.logical_or(not_first, cur_half), s3[:n_first], NEG_INF)
                s3 = masked if n_first == len(group) else jnp.concatenate([masked, s3[n_first:]], axis=0)
            m3 = jnp.max(s3, axis=-1, keepdims=True)
            p3 = jnp.exp2(s3 - m3)
            l3 = jnp.sum(p3, axis=-1, keepdims=True)
            o3 = jnp.einsum("bqk,bkd->bqd", p3.astype(BF16), v3, preferred_element_type=F32) / l3
            lse3 = jnp.broadcast_to(m3 + jnp.log2(l3), o3.shape)
            for i, r in enumerate(q_rows):
                o_s[p_idx][r, :] = o3[i]
                lse_s[p_idx][r, :] = lse3[i]

    lses = [ref[...] for ref in lse_s]
    m = functools.reduce(jnp.maximum, lses)
    ws = [jnp.exp2(x - m) for x in lses]
    num = functools.reduce(lambda a, b: a + b, [w * ref[...] for w, ref in zip(ws, o_s)])
    den = functools.reduce(lambda a, b: a + b, ws)
    o_ref[...] = (num / den * _silu(gb_ref[...])).astype(o_ref.dtype)


def _dil_prompt(z, rel_bias, *, col0, tile):
    s = z.shape[0]
    hd = LANES
    n_pat = len(B_DILATIONS)
    qc, kc, vc, gc = (col0 + i * H_B for i in range(4))
    prev = lambda t: jnp.maximum(t - 1, 0)
    blk = lambda shape, fn: pl.BlockSpec(shape, fn)
    return pl.pallas_call(
        _dil_prompt_kernel,
        grid=(H_B, s // tile),
        in_specs=[blk((tile, hd), lambda h, t: (t, qc + h)),
                  blk((tile, hd), lambda h, t: (t, kc + h)),
                  blk((tile, hd), lambda h, t: (prev(t), kc + h)),
                  blk((tile, hd), lambda h, t: (t, vc + h)),
                  blk((tile, hd), lambda h, t: (prev(t), vc + h)),
                  blk((tile, hd), lambda h, t: (t, gc + h)),
                  blk((n_pat, B_BLOCK, 2 * B_BLOCK), lambda h, t: (0, 0, 0)),
                  pl.BlockSpec(memory_space=pltpu.SMEM)],
        out_specs=blk((tile, hd), lambda h, t: (t, h)),
        out_shape=jax.ShapeDtypeStruct((s, H_B * hd), BF16),
        scratch_shapes=[pltpu.VMEM((n_pat, B_BLOCK, 2 * B_BLOCK), F32)]
                       + [pltpu.VMEM((tile, hd), F32)] * (2 * n_pat),
        compiler_params=_params("parallel", "arbitrary"),
        name="dilated_prompt",
    )(z, z, z, z, z, z, jnp.asarray(_dil_prompt_buckets()), rel_bias)


def _dil_sample_kernel(q_ref, kn_ref, vn_ref, gb_ref, *rest):
    n_pat = len(B_DILATIONS)
    k_refs = rest[:n_pat]
    v_refs = rest[n_pat:2 * n_pat]
    bias_ref, bias0_ref, o_ref = rest[2 * n_pat:]
    n_h, hd = q_ref.shape[1:]
    q = q_ref[0] * hd ** -0.5
    qb = q.astype(BF16)
    s_new = jnp.sum(q * kn_ref[0], axis=1, keepdims=True) + bias0_ref[...]
    s = [_dot_nt(qb, k_refs[g][0].reshape(B_BLOCK * n_h, hd).astype(BF16)) + bias_ref[g]
         for g in range(n_pat)]
    m = s_new
    for sg in s:
        m = jnp.maximum(m, jnp.max(sg, axis=1, keepdims=True))
    p_new = n_pat * jnp.exp(s_new - m)
    den = p_new
    num = p_new * vn_ref[0]
    for g in range(n_pat):
        p = jnp.exp(s[g] - m)
        den = den + jnp.sum(p, axis=1, keepdims=True)
        num = num + _dot(p.astype(BF16), v_refs[g][0].reshape(B_BLOCK * n_h, hd).astype(BF16))
    o_ref[0] = (num / den * _silu(gb_ref[0])).astype(o_ref.dtype)


def _dil_sample_bias(rel_bias):
    n_h = rel_bias.shape[1]
    j = np.arange(B_BLOCK)
    own = (np.arange(B_BLOCK * n_h)[None, :] % n_h) == np.arange(n_h)[:, None]
    tabs = []
    for dil in B_DILATIONS:
        flat = rel_bias[_t5_bucket((B_BLOCK - j) * dil)].reshape(1, B_BLOCK * n_h)
        tabs.append(jnp.where(own, flat, NEG_INF))
    return jnp.stack(tabs), rel_bias[_t5_bucket(np.zeros((1,), np.int64))].T


def _dil_sample(zh, k_buf, v_buf, bias, bias0, *, row0):
    bsz, wb, n_h, hd = k_buf.shape
    views, specs = [], []
    for buf in (k_buf, v_buf):
        for d in B_DILATIONS:
            assert wb % (d * B_BLOCK) == 0
            views.append(buf.reshape(bsz, wb // d, d * n_h, hd))
            last = wb // d // B_BLOCK - 1
            specs.append(pl.BlockSpec((1, B_BLOCK, n_h, hd), lambda b, last=last: (b, last, 0, 0)))
    zspec = lambda c: pl.BlockSpec((1, n_h, hd), lambda b: (b, c, 0))
    n_pat = len(B_DILATIONS)
    return pl.pallas_call(
        _dil_sample_kernel,
        grid=(bsz,),
        in_specs=[zspec(row0), zspec(row0 + 1), zspec(row0 + 2), zspec(row0 + 3)] + specs
                 + [pl.BlockSpec((n_pat, n_h, B_BLOCK * n_h), lambda b: (0, 0, 0)),
                    pl.BlockSpec((n_h, 1), lambda b: (0, 0))],
        out_specs=pl.BlockSpec((1, n_h, hd), lambda b: (b, 0, 0)),
        out_shape=jax.ShapeDtypeStruct((bsz, n_h, hd), BF16),
        compiler_params=_params("parallel"),
        name="dilated_sample",
    )(zh, zh, zh, zh, *views, bias, bias0)


SHIFT_SLOTS = 4
SHIFT_ROWS = 1024


def _shift_kernel(kb_ref, vb_ref, kn_ref, vn_ref, ko_ref, vo_ref, stage, new_stage, in_sem, out_sem, new_sem):
    bsz, wb = kb_ref.shape[0], kb_ref.shape[1]
    n_slots, rows = stage.shape[0], stage.shape[1]

    new_in = [pltpu.make_async_copy(new, new_stage.at[i], new_sem.at[i])
              for i, new in enumerate((kn_ref, vn_ref))]
    new_out = [pltpu.make_async_copy(new_stage.at[i], out.at[:, pl.ds(wb - 1, 1)], new_sem.at[i])
               for i, out in enumerate((ko_ref, vo_ref))]
    for c in new_in:
        c.start()

    chunks = []
    for buf, out in ((kb_ref, ko_ref), (vb_ref, vo_ref)):
        for b in range(bsz):
            r = 1
            while r < wb:
                n = min(rows, wb - r)
                chunks.append((buf, out, b, r, n))
                r += n

    def copy_in(idx):
        buf, _, b, r, n = chunks[idx]
        slot = idx % n_slots
        return pltpu.make_async_copy(buf.at[b, pl.ds(r, n)], stage.at[slot, pl.ds(0, n)], in_sem.at[slot])

    def copy_out(idx):
        _, out, b, r, n = chunks[idx]
        slot = idx % n_slots
        return pltpu.make_async_copy(stage.at[slot, pl.ds(0, n)], out.at[b, pl.ds(r - 1, n)], out_sem.at[slot])

    ahead = n_slots - 1
    for idx in range(min(ahead, len(chunks))):
        copy_in(idx).start()
    for idx in range(len(chunks)):
        copy_in(idx).wait()
        copy_out(idx).start()
        if idx >= 1:
            copy_out(idx - 1).wait()
        if idx + ahead < len(chunks):
            copy_in(idx + ahead).start()
    copy_out(len(chunks) - 1).wait()

    for c in new_in:
        c.wait()
    for c in new_out:
        c.start()
    for c in new_out:
        c.wait()


def _shift_caches(k_buf, v_buf, k_new, v_new):
    bsz, wb = k_buf.shape[:2]
    rows = min(SHIFT_ROWS, wb)
    any_spec = pl.BlockSpec(memory_space=pl.ANY)
    return pl.pallas_call(
        _shift_kernel,
        in_specs=[any_spec] * 4,
        out_specs=[any_spec] * 2,
        out_shape=[jax.ShapeDtypeStruct(k_buf.shape, k_buf.dtype), jax.ShapeDtypeStruct(v_buf.shape, v_buf.dtype)],
        scratch_shapes=[pltpu.VMEM((SHIFT_SLOTS, rows) + k_buf.shape[2:], k_buf.dtype),
                        pltpu.VMEM((2,) + k_new.shape, k_new.dtype),
                        pltpu.SemaphoreType.DMA((SHIFT_SLOTS,)),
                        pltpu.SemaphoreType.DMA((SHIFT_SLOTS,)),
                        pltpu.SemaphoreType.DMA((2,))],
        compiler_params=pltpu.CompilerParams(vmem_limit_bytes=VMEM_LIMIT_BYTES),
        name="shift_caches",
    )(k_buf, v_buf, k_new, v_new)


def _row_tile(m, cap):
    return m if m <= cap else cap


def _even_weights(w_in, w_out, conv_w, conv_b, gate_w, gate_b, lam, rel_bias):
    d_a = conv_w.shape[1]
    gw = jnp.concatenate([gate_w[0], gate_w[1]], axis=-1).astype(BF16)
    return dict(w_in=w_in.astype(BF16), wa=w_out[:d_a].astype(BF16), wb=w_out[d_a:].astype(BF16),
                conv_w=conv_w, conv_b=conv_b[None], gw=gw, gate_b=gate_b, lam=lam[None],
                rel_bias=rel_bias, bias_s=_dil_sample_bias(rel_bias), d_a=d_a)


def _even_prompt(x, g_pre, g_post, w):
    s = x.shape[0]
    d_a = w["d_a"]
    d_b = w["wb"].shape[0]
    z = _norm_proj(x, g_pre, w["w_in"], tm=_row_tile(s, 1024), tn=1024)
    ya, h_last, conv_new = _rglru_prompt(z, w["conv_w"], w["conv_b"], w["gw"], w["gate_b"], w["lam"],
                                         d_a=d_a, tb=256)
    yb = _dil_prompt(z, w["rel_bias"], col0=2 * d_a // LANES, tile=B_DILATIONS[-1] * B_BLOCK)
    y = _out_proj(ya, yb, w["wa"], w["wb"], g_post, x, tm=_row_tile(s, 512))
    wbp = min(B_DILATIONS[-1] * B_BLOCK, s)
    k_state = z[s - wbp:, 2 * d_a + d_b:2 * d_a + 2 * d_b]
    v_state = z[s - wbp:, 2 * d_a + 2 * d_b:2 * d_a + 3 * d_b]
    return y, h_last, conv_new, k_state, v_state


def _even_sample(x, g_pre, g_post, w, h0, conv0, k_buf, v_buf, *, shift_now):
    bsz = x.shape[0]
    d_a = w["d_a"]
    d_b = w["wb"].shape[0]
    z = _norm_proj(x, g_pre, w["w_in"], tm=bsz, tn=1024)
    ya, h_new, conv_new_t = _rglru_sample(z, jnp.swapaxes(conv0, 0, 1), h0, w["conv_w"], w["conv_b"],
                                          w["gw"], w["gate_b"], w["lam"], d_a=d_a)
    bias_s, bias0 = w["bias_s"]
    yb = _dil_sample(z.reshape(bsz, -1, k_buf.shape[-1]), k_buf, v_buf, bias_s, bias0, row0=2 * d_a // d_b)
    k_new = z[:, 2 * d_a + d_b:2 * d_a + 2 * d_b].reshape(bsz, 1, *k_buf.shape[2:])
    v_new = z[:, 2 * d_a + 2 * d_b:2 * d_a + 3 * d_b].reshape(bsz, 1, *v_buf.shape[2:])
    k_out, v_out = _shift_caches(k_buf, v_buf, k_new, v_new) if shift_now else (k_new, v_new)
    y = _out_proj(ya, yb.reshape(bsz, d_b), w["wa"], w["wb"], g_post, x, tm=bsz)
    return y, h_new, jnp.swapaxes(conv_new_t, 0, 1), k_out, v_out


def _dot_f32(a, b):
    return jnp.dot(a, b, precision=lax.Precision.HIGHEST, preferred_element_type=F32)


def _mlstm_prompt_kernel(q_ref, k_ref, v_ref, o_ref, g_ref, gates_ref, gbias_ref,
                         y_ref, c_out, n_out, m_out, c_s, n_s, m_s):
    lc = q_ref.shape[0]
    hd = q_ref.shape[1] // H_C
    kscale = hd ** -0.5

    @pl.when(pl.program_id(0) == 0)
    def _():
        c_s[...] = jnp.zeros_like(c_s)
        n_s[...] = jnp.zeros_like(n_s)
        m_s[...] = jnp.zeros_like(m_s)

    gpre = gates_ref[...] + gbias_ref[...]
    lane = lax.broadcasted_iota(jnp.int32, gpre.shape, 1)
    ti = lax.broadcasted_iota(jnp.int32, (lc, lc), 0)
    si = lax.broadcasted_iota(jnp.int32, (lc, lc), 1)
    causal = si <= ti
    bcum = _dot_f32(causal.astype(F32), _log_sigmoid(gpre))
    mix = jnp.where(lane < H_C, gpre, bcum)
    mix_t = mix.T

    for h in range(H_C):
        sl = slice(h * hd, (h + 1) * hd)
        i_col, b_col = mix[:, h:h + 1], mix[:, H_C + h:H_C + h + 1]
        i_row, b_row = mix_t[h:h + 1, :], mix_t[H_C + h:H_C + h + 1, :]
        m0 = m_s[h:h + 1, 0:1]
        q = q_ref[:, sl]
        qb = q.astype(BF16)
        ks = k_ref[:, sl] * kscale
        kb = ks.astype(BF16)
        v = v_ref[:, sl]
        c0 = c_s[h]
        n0 = n_s[h:h + 1, :]

        dm = jnp.where(causal, b_col - b_row + i_row, NEG_INF)
        inter = b_col + m0
        mt = jnp.maximum(inter, jnp.max(dm, axis=1, keepdims=True))
        wqk = jnp.exp(dm - mt) * _dot_nt(qb, kb)
        g = jnp.exp(inter - mt)
        num = _dot(wqk.astype(BF16), v.astype(BF16)) + g * _dot_nt(qb, c0.astype(BF16))
        den = jnp.sum(wqk, axis=1, keepdims=True) + g * jnp.sum(q * n0, axis=1, keepdims=True)
        hh = num / jnp.maximum(jnp.abs(den), jnp.exp(-mt))
        y_ref[:, sl] = (_sigmoid(o_ref[:, sl]) * hh * _silu(g_ref[:, sl])).astype(y_ref.dtype)

        b_last = b_col[lc - 1:lc, :]
        wlast = b_last - b_col + i_col
        m_new = jnp.maximum(b_last + m0, jnp.max(wlast, axis=0, keepdims=True))
        ws = jnp.exp(wlast - m_new)
        g_last = jnp.exp(b_last + m0 - m_new)
        c_new = g_last * c0 + _dot_tn((v * ws).astype(BF16), kb)
        n_new = g_last * n0 + jnp.sum(ws * ks, axis=0, keepdims=True)
        c_s[h] = c_new
        n_s[h:h + 1, :] = n_new
        m_s[h:h + 1, :] = jnp.broadcast_to(m_new, (1, LANES))
        c_out[h] = c_new
        n_out[h:h + 1, :] = n_new
    m_out[...] = m_s[...]


def _mlstm_prompt(z, gates, gbias, *, d_c):
    s = z.shape[0]
    lc = C_CHUNK if s % C_CHUNK == 0 else s
    hd = d_c // H_C
    full = lambda shape: pl.BlockSpec(shape, lambda c: (0,) * len(shape))
    sec = lambda j: pl.BlockSpec((lc, d_c), lambda c, j=j: (c, j))
    return pl.pallas_call(
        _mlstm_prompt_kernel,
        grid=(s // lc,),
        in_specs=[sec(0), sec(1), sec(2), sec(3), sec(4),
                  pl.BlockSpec((lc, LANES), lambda c: (c, 0)), full((1, LANES))],
        out_specs=[pl.BlockSpec((lc, d_c), lambda c: (c, 0)),
                   full((H_C, hd, hd)), full((H_C, hd)), full((8, LANES))],
        out_shape=[jax.ShapeDtypeStruct((s, d_c), BF16),
                   jax.ShapeDtypeStruct((H_C, hd, hd), F32),
                   jax.ShapeDtypeStruct((H_C, hd), F32),
                   jax.ShapeDtypeStruct((8, LANES), F32)],
        scratch_shapes=[pltpu.VMEM((H_C, hd, hd), F32), pltpu.VMEM((H_C, hd), F32),
                        pltpu.VMEM((8, LANES), F32)],
        compiler_params=_params("arbitrary"),
        name="mlstm_prompt",
    )(z, z, z, z, z, gates, gbias)


def _mlstm_sample_kernel(q_ref, k_ref, v_ref, o_ref, g_ref, gates_ref, gbias_ref, c_ref, n_ref, m_ref,
                         y_ref, c_out, n_out, m_out, lf_out):
    hd = q_ref.shape[-1] // H_C
    kscale = hd ** -0.5
    gpre = gates_ref[0] + gbias_ref[...]
    logf = _log_sigmoid(gpre)
    lf_out[0] = logf
    for h in range(H_C):
        sl = slice(h * hd, (h + 1) * hd)
        i_g = gpre[:, h:h + 1]
        f_g = logf[:, H_C + h:H_C + h + 1]
        m0 = m_ref[0, :, h:h + 1]
        q = q_ref[0, :, sl]
        ks = k_ref[0, :, sl] * kscale
        v = v_ref[0, :, sl]
        c0 = c_ref[0, h]
        n0 = n_ref[0, h:h + 1, :]

        inter = f_g + m0
        mt = jnp.maximum(inter, i_g)
        wqk = jnp.exp(i_g - mt) * jnp.sum(q * ks, axis=1, keepdims=True)
        g = jnp.exp(inter - mt)
        cq = _dot_nt(jnp.broadcast_to(q, (8, hd)).astype(BF16), c0.astype(BF16))[0:1, :]
        num = wqk * v + g * cq
        den = wqk + g * jnp.sum(n0 * q, axis=1, keepdims=True)
        hh = num / jnp.maximum(jnp.abs(den), jnp.exp(-mt))
        y_ref[0, :, sl] = (_sigmoid(o_ref[0, :, sl]) * hh * _silu(g_ref[0, :, sl])).astype(y_ref.dtype)

        m_new = jnp.maximum(f_g + m0, i_g)
        ws = jnp.exp(i_g - m_new)
        g_last = jnp.exp(f_g + m0 - m_new)
        v_col = jnp.broadcast_to(v, (LANES, hd)).T[:, 0:1]
        c_out[0, h] = g_last * c0 + ws * (v_col * ks)
        n_out[0, h:h + 1, :] = g_last * n0 + ws * ks
        m_out[0, :, h:h + 1] = m_new


def _mlstm_sample(z3, gates3, gbias, c0, n0, m0):
    bsz, n_h, hd = n0.shape
    d_c = n_h * hd
    sec = lambda j: pl.BlockSpec((1, 1, d_c), lambda b, j=j: (b, 0, j))
    return pl.pallas_call(
        _mlstm_sample_kernel,
        grid=(bsz,),
        in_specs=[sec(0), sec(1), sec(2), sec(3), sec(4),
                  pl.BlockSpec((1, 1, LANES), lambda b: (b, 0, 0)),
                  pl.BlockSpec((1, LANES), lambda b: (0, 0)),
                  pl.BlockSpec((1, n_h, hd, hd), lambda b: (b, 0, 0, 0)),
                  pl.BlockSpec((1, n_h, hd), lambda b: (b, 0, 0)),
                  pl.BlockSpec((1, 1, n_h), lambda b: (b, 0, 0))],
        out_specs=[pl.BlockSpec((1, 1, d_c), lambda b: (b, 0, 0)),
                   pl.BlockSpec((1, n_h, hd, hd), lambda b: (b, 0, 0, 0)),
                   pl.BlockSpec((1, n_h, hd), lambda b: (b, 0, 0)),
                   pl.BlockSpec((1, 1, n_h), lambda b: (b, 0, 0)),
                   pl.BlockSpec((1, 1, LANES), lambda b: (b, 0, 0))],
        out_shape=[jax.ShapeDtypeStruct((bsz, 1, d_c), BF16),
                   jax.ShapeDtypeStruct(c0.shape, F32),
                   jax.ShapeDtypeStruct(n0.shape, F32),
                   jax.ShapeDtypeStruct(m0.shape, F32),
                   jax.ShapeDtypeStruct((bsz, 1, LANES), F32)],
        compiler_params=_params("parallel"),
        name="mlstm_sample",
    )(z3, z3, z3, z3, z3, gates3, gbias, c0, n0, m0)


N_DECAY_PARTS = 3


def _fox_gates_kernel(gates_ref, gbias_ref, logf_ref, e_ref, carry_s):
    tb = gates_ref.shape[0]

    @pl.when(pl.program_id(0) == 0)
    def _():
        carry_s[...] = jnp.zeros_like(carry_s)

    logf = _log_sigmoid(gates_ref[...] + gbias_ref[...])
    logf_ref[...] = logf
    lower = (lax.broadcasted_iota(jnp.int32, (tb, tb), 1)
             <= lax.broadcasted_iota(jnp.int32, (tb, tb), 0)).astype(F32)
    cum = _dot_f32(lower, logf) + carry_s[0:1, :]
    carry_s[...] = jnp.broadcast_to(cum[tb - 1:tb, :], carry_s.shape)
    lane = lax.broadcasted_iota(jnp.int32, (tb, LANES), 1)
    for h in range(H_D):
        rest = jnp.broadcast_to(cum[:, 2 * H_C + h:2 * H_C + h + 1] * (-LOG2E), (tb, LANES))
        e = jnp.zeros((tb, LANES), F32)
        for part in range(N_DECAY_PARTS):
            piece = rest.astype(BF16).astype(F32)
            e = jnp.where(lane == part, piece, e)
            rest = rest - piece
        e_ref[h] = e.astype(BF16)


def _fox_gates(gates, gbias, *, tb):
    s = gates.shape[0]
    return pl.pallas_call(
        _fox_gates_kernel,
        grid=(s // tb,),
        in_specs=[pl.BlockSpec((tb, LANES), lambda i: (i, 0)), pl.BlockSpec((1, LANES), lambda i: (0, 0))],
        out_specs=[pl.BlockSpec((tb, LANES), lambda i: (i, 0)),
                   pl.BlockSpec((H_D, tb, LANES), lambda i: (0, i, 0))],
        out_shape=[jax.ShapeDtypeStruct((s, LANES), F32), jax.ShapeDtypeStruct((H_D, s, LANES), BF16)],
        scratch_shapes=[pltpu.VMEM((8, LANES), F32)],
        compiler_params=_params("arbitrary"),
        name="fox_gates",
    )(gates, gbias)


BG_PARTS = 4


def _bg_parts(wb):
    rows = -(-(wb - 1) // BG_PARTS)
    parts, r = [], 1
    while r < wb:
        n = min(rows, wb - r)
        parts.append((r, n))
        r += n
    return parts, rows


class _BackgroundShift:
    def __init__(self, t, n_steps, refs, scratch):
        self.t, self.n_steps = t, n_steps
        self.kb, self.vb, self.kn, self.vn, self.ko, self.vo = refs
        self.stage, self.new_stage, self.in_sem, self.out_sem, self.new_sem = scratch
        self.bsz, self.wb = self.kb.shape[:2]
        self.parts, _ = _bg_parts(self.wb)
        self.n_pairs = 2 * self.bsz
        self.per_step = -(-self.n_pairs // n_steps)

    def _slot(self, step, u):
        return ((step % 2) * self.per_step + u) * len(self.parts)

    def _in(self, buf, b, p, slot):
        r0, n = self.parts[p]
        return pltpu.make_async_copy(buf.at[b, pl.ds(r0, n)], self.stage.at[slot + p, pl.ds(0, n)],
                                     self.in_sem.at[slot + p])

    def _out(self, out, b, p, slot):
        r0, n = self.parts[p]
        return pltpu.make_async_copy(self.stage.at[slot + p, pl.ds(0, n)], out.at[b, pl.ds(r0 - 1, n)],
                                     self.out_sem.at[slot + p])

    def _pair(self, step, u, cond, fn):
        pair = step * self.per_step + u
        slot = self._slot(step, u)

        @pl.when(jnp.logical_and(cond, pair < self.bsz))
        def _():
            fn(self.kb, self.ko, pair, slot)

        @pl.when(jnp.logical_and(cond, jnp.logical_and(pair >= self.bsz, pair < self.n_pairs)))
        def _():
            fn(self.vb, self.vo, pair - self.bsz, slot)

    def _new_in(self):
        return [pltpu.make_async_copy(new, self.new_stage.at[i], self.new_sem.at[i])
                for i, new in enumerate((self.kn, self.vn))]

    def _new_out(self):
        return [pltpu.make_async_copy(self.new_stage.at[i], out.at[:, pl.ds(self.wb - 1, 1)], self.new_sem.at[i])
                for i, out in enumerate((self.ko, self.vo))]

    def begin_step(self):
        t = self.t
        n_parts = range(len(self.parts))

        @pl.when(t == 0)
        def _():
            for c in self._new_in():
                c.start()

        def reads_to_writes(buf, out, b, slot):
            for p in n_parts:
                self._in(buf, b, p, slot).wait()
                self._out(out, b, p, slot).start()

        def wait_writes(buf, out, b, slot):
            for p in n_parts:
                self._out(out, b, p, slot).wait()

        def start_reads(buf, out, b, slot):
            for p in n_parts:
                self._in(buf, b, p, slot).start()

        for u in range(self.per_step):
            self._pair(t - 1, u, t >= 1, reads_to_writes)
            self._pair(t - 2, u, t >= 2, wait_writes)
            self._pair(t, u, t >= 0, start_reads)

    def end_step(self):
        t = self.t
        last = t == self.n_steps - 1
        n_parts = range(len(self.parts))

        def finish(buf, out, b, slot):
            for p in n_parts:
                self._in(buf, b, p, slot).wait()
                self._out(out, b, p, slot).start()
            for p in n_parts:
                self._out(out, b, p, slot).wait()

        def wait_writes(buf, out, b, slot):
            for p in n_parts:
                self._out(out, b, p, slot).wait()

        for u in range(self.per_step):
            self._pair(t - 1, u, jnp.logical_and(last, t >= 1), wait_writes)
            self._pair(t, u, last, finish)

        @pl.when(last)
        def _():
            for c in self._new_in():
                c.wait()
            for c in self._new_out():
                c.start()
            for c in self._new_out():
                c.wait()


def _fox_prompt_kernel(q_ref, k_ref, v_ref, g_ref, e_ref, *rest, tk, grid, with_shift):
    if with_shift:
        shift_in, rest = rest[:4], rest[4:]
        y_ref, shift_out, rest = rest[0], rest[1:3], rest[3:]
        ka_s, vt_s, m_s, l_s, acc_s, sa_s, sb_s = rest[:7]
        shift = _BackgroundShift(pl.program_id(0) * grid[1] + pl.program_id(1), grid[0] * grid[1],
                                 tuple(shift_in) + tuple(shift_out), rest[7:])
        shift.begin_step()
    else:
        y_ref, ka_s, vt_s, m_s, l_s, acc_s, sa_s, sb_s = rest
    tq, hd = q_ref.shape
    s_len = k_ref.shape[0]
    i = pl.program_id(1)
    assert tq == 2 * tk

    @pl.when(i == 0)
    def _():
        def prep(c, carry):
            st = pl.multiple_of(c * tk, tk)
            ka_s[pl.ds(st, tk), 0:hd] = k_ref[pl.ds(st, tk), :].astype(BF16)
            ka_s[pl.ds(st, tk), hd:2 * hd] = e_ref[0, pl.ds(st, tk), :]
            vt_s[:, pl.ds(st, tk)] = v_ref[pl.ds(st, tk), :].T.astype(BF16)
            return carry
        lax.fori_loop(0, s_len // tk, prep, 0)

    lane = lax.broadcasted_iota(jnp.int32, (tq, hd), 1)
    qa = jnp.concatenate([(q_ref[...] * (hd ** -0.5 * LOG2E)).astype(BF16),
                          jnp.where(lane < N_DECAY_PARTS, 1.0, 0.0).astype(BF16)], axis=1)
    m_s[...] = jnp.full_like(m_s, NEG_INF)
    l_s[...] = jnp.zeros_like(l_s)
    acc_s[...] = jnp.zeros_like(acc_s)

    def scores(blk, dst):
        st = pl.multiple_of(blk * tk, tk)
        dst[...] = _dot_nt(ka_s[pl.ds(st, tk), :], qa)

    def consume(src, blk, diag_offset):
        st = pl.multiple_of(blk * tk, tk)
        s = src[...]
        if diag_offset is not None:
            s = jnp.where(lax.broadcasted_iota(jnp.int32, s.shape, 0) + diag_offset
                          <= lax.broadcasted_iota(jnp.int32, s.shape, 1), s, NEG_INF)
        m_o = m_s[0:1, :]
        m_n = jnp.maximum(m_o, jnp.max(s, axis=0, keepdims=True))
        al = jnp.exp2(m_o - m_n)
        p = jnp.exp2(s - m_n)
        l_s[0:1, :] = l_s[0:1, :] * al + jnp.sum(p, axis=0, keepdims=True)
        acc_s[...] = acc_s[...] * al + _dot(vt_s[:, pl.ds(st, tk)], p.astype(BF16))
        m_s[0:1, :] = m_n

    scores(0, sa_s)

    def body(t, carry):
        scores(2 * t + 1, sb_s)
        consume(sa_s, 2 * t, None)
        scores(2 * t + 2, sa_s)
        consume(sb_s, 2 * t + 1, None)
        return carry

    lax.fori_loop(0, i, body, 0)
    scores(2 * i + 1, sb_s)
    consume(sa_s, 2 * i, 0)
    consume(sb_s, 2 * i + 1, tk)
    y_t = acc_s[...] / l_s[0:1, :]
    y_ref[...] = (y_t.T * _silu(g_ref[...])).astype(y_ref.dtype)
    if with_shift:
        shift.end_step()


def _fox_prompt(z, e, shift=None, *, col0, tq, tk):
    s = z.shape[0]
    hd = LANES
    qc, kc, vc, gc = (col0 + i * H_D for i in range(4))
    grid = (H_D, s // tq)
    in_specs = [pl.BlockSpec((tq, hd), lambda h, i: (i, qc + h)),
                pl.BlockSpec((s, hd), lambda h, i: (0, kc + h)),
                pl.BlockSpec((s, hd), lambda h, i: (0, vc + h)),
                pl.BlockSpec((tq, hd), lambda h, i: (i, gc + h)),
                pl.BlockSpec((1, s, hd), lambda h, i: (h, 0, 0))]
    out_specs = [pl.BlockSpec((tq, hd), lambda h, i: (i, h))]
    out_shape = [jax.ShapeDtypeStruct((s, H_D * hd), BF16)]
    scratch = [pltpu.VMEM((s, 2 * hd), BF16), pltpu.VMEM((hd, s), BF16),
               pltpu.VMEM((8, tq), F32), pltpu.VMEM((8, tq), F32), pltpu.VMEM((hd, tq), F32),
               pltpu.VMEM((tk, tq), F32), pltpu.VMEM((tk, tq), F32)]
    args = [z, z, z, z, e]
    if shift is not None:
        k_buf, v_buf, k_new, v_new = shift
        bsz, wb = k_buf.shape[:2]
        parts, rows = _bg_parts(wb)
        per_step = -(-2 * bsz // (grid[0] * grid[1]))
        n_slots = 2 * per_step * len(parts)
        any_spec = pl.BlockSpec(memory_space=pl.ANY)
        in_specs += [any_spec] * 4
        out_specs += [any_spec] * 2
        out_shape += [jax.ShapeDtypeStruct(k_buf.shape, k_buf.dtype), jax.ShapeDtypeStruct(v_buf.shape, v_buf.dtype)]
        scratch += [pltpu.VMEM((n_slots, rows) + k_buf.shape[2:], k_buf.dtype),
                    pltpu.VMEM((2,) + k_new.shape, k_new.dtype),
                    pltpu.SemaphoreType.DMA((n_slots,)), pltpu.SemaphoreType.DMA((n_slots,)),
                    pltpu.SemaphoreType.DMA((2,))]
        args += [k_buf, v_buf, k_new, v_new]
    outs = pl.pallas_call(
        functools.partial(_fox_prompt_kernel, tk=tk, grid=grid, with_shift=shift is not None),
        grid=grid,
        in_specs=in_specs, out_specs=out_specs, out_shape=out_shape,
        scratch_shapes=scratch,
        compiler_params=_params("arbitrary", "arbitrary"),
        name="fox_prompt",
    )(*args)
    return outs if shift is not None else outs[0]


def _page_sums_kernel(lf_ref, suf_ref, tot_ref):
    x = lf_ref[...]
    width = x.shape[1]
    lane = lax.broadcasted_iota(jnp.int32, x.shape, 1)
    inc, cyc = x, x
    step = H_D
    while step < width:
        shifted = pltpu.roll(inc, width - step, axis=1)
        inc = inc + jnp.where(lane < width - step, shifted, 0.0)
        cyc = cyc + pltpu.roll(cyc, width - step, axis=1)
        step *= 2
    suf_ref[...] = inc - x
    tot_ref[...] = cyc


def _page_sums(lf_flat, *, rows):
    n_pool, width = lf_flat.shape
    spec = pl.BlockSpec((rows, width), lambda i: (i, 0))
    return pl.pallas_call(
        _page_sums_kernel,
        grid=(n_pool // rows,),
        in_specs=[spec], out_specs=[spec, spec],
        out_shape=[jax.ShapeDtypeStruct(lf_flat.shape, F32)] * 2,
        compiler_params=_params("parallel"),
        name="page_sums",
    )(lf_flat)


def _fox_sample_kernel(pt_ref, q_ref, kn_ref, vn_ref, g_ref, lfn_ref, fb_ref, *rest, pages_per_step):
    g_pages = pages_per_step
    k_refs = rest[:g_pages]
    v_refs = rest[g_pages:2 * g_pages]
    suf_refs = rest[2 * g_pages:3 * g_pages]
    tot_refs = rest[3 * g_pages:4 * g_pages]
    y_ref, m_s, l_s, acc_s, carry_s = rest[4 * g_pages:]
    j = pl.program_id(1)
    n_h, hd = q_ref.shape[1:]
    width = k_refs[0].shape[1]
    scale = hd ** -0.5
    q = q_ref[0] * scale
    qb = q.astype(BF16)
    own = (lax.broadcasted_iota(jnp.int32, (n_h, width), 1) % n_h
           == lax.broadcasted_iota(jnp.int32, (n_h, width), 0))

    @pl.when(j == 0)
    def _():
        carry_s[...] = _log_sigmoid(lfn_ref[0] + fb_ref[...])
        s_new = jnp.sum(q * kn_ref[0], axis=1, keepdims=True)
        m_s[...] = jnp.broadcast_to(s_new, m_s.shape)
        l_s[...] = jnp.ones_like(l_s)
        acc_s[...] = vn_ref[0]

    carry = carry_s[...]
    scores = []
    for i in range(g_pages):
        bias = suf_refs[i][0] + carry
        carry = carry + tot_refs[i][0]
        s = _dot_nt(qb, k_refs[i][0].astype(BF16)) + bias
        scores.append(jnp.where(own, s, NEG_INF))
    carry_s[...] = carry

    m_o = m_s[:, 0:1]
    m_n = m_o
    for s in scores:
        m_n = jnp.maximum(m_n, jnp.max(s, axis=1, keepdims=True))
    al = jnp.exp(m_o - m_n)
    l_n = l_s[:, 0:1] * al
    acc = acc_s[...] * al
    for i in range(g_pages):
        p = jnp.exp(scores[i] - m_n)
        l_n = l_n + jnp.sum(p, axis=1, keepdims=True)
        acc = acc + _dot(p.astype(BF16), v_refs[i][0].astype(BF16))
    m_s[...] = jnp.broadcast_to(m_n, m_s.shape)
    l_s[...] = jnp.broadcast_to(l_n, l_s.shape)
    acc_s[...] = acc

    @pl.when(j == pl.num_programs(1) - 1)
    def _():
        y_ref[0] = (acc / l_n * _silu(g_ref[0])).astype(y_ref.dtype)


def _fox_sample(page_table, zh, lfn_row, fbias_row, k_pool, v_pool, suf, tot, *, row0, pages_per_step):
    bsz, n_pages = page_table.shape
    _, width, hd = k_pool.shape
    n_h = H_D
    g_pages = pages_per_step
    assert n_pages % g_pages == 0
    zspec = lambda c: pl.BlockSpec((1, n_h, hd), lambda b, j, pt: (b, c, 0))

    def page_of(i):
        return lambda b, j, pt: pt[b, n_pages - 1 - (j * g_pages + i)]

    kv_specs = [pl.BlockSpec((1, width, hd), lambda b, j, pt, f=page_of(i): (f(b, j, pt), 0, 0))
                for i in range(g_pages)]
    row_specs = [pl.BlockSpec((1, 1, width), lambda b, j, pt, f=page_of(i): (f(b, j, pt), 0, 0))
                 for i in range(g_pages)]
    grid_spec = pltpu.PrefetchScalarGridSpec(
        num_scalar_prefetch=1,
        grid=(bsz, n_pages // g_pages),
        in_specs=[zspec(row0), zspec(row0 + 1), zspec(row0 + 2), zspec(row0 + 3),
                  pl.BlockSpec((1, 1, width), lambda b, j, pt: (b, 0, 0)),
                  pl.BlockSpec((1, width), lambda b, j, pt: (0, 0))]
                 + kv_specs + kv_specs + row_specs + row_specs,
        out_specs=pl.BlockSpec((1, n_h, hd), lambda b, j, pt: (b, 0, 0)),
        scratch_shapes=[pltpu.VMEM((n_h, hd), F32)] * 3 + [pltpu.VMEM((1, width), F32)],
    )
    return pl.pallas_call(
        functools.partial(_fox_sample_kernel, pages_per_step=g_pages),
        grid_spec=grid_spec,
        out_shape=jax.ShapeDtypeStruct((bsz, n_h, hd), BF16),
        compiler_params=_params("parallel", "arbitrary"),
        name="fox_sample",
    )(page_table, zh, zh, zh, zh, lfn_row, fbias_row, *([k_pool] * g_pages), *([v_pool] * g_pages),
      *([suf] * g_pages), *([tot] * g_pages))


def _odd_weights(w_in, w_out, c_gate_b, d_f_b):
    d_d = H_D * LANES
    d_c = w_out.shape[0] - d_d
    g0 = 5 * d_c
    g1 = g0 + 2 * H_C + 4 * d_d
    d = w_in.shape[0]
    w_main = jnp.concatenate([w_in[:, :g0], w_in[:, g0 + 2 * H_C:g1]], axis=1).astype(BF16)
    n_g = 2 * H_C + H_D
    w_gate = jnp.concatenate([w_in[:, g0:g0 + 2 * H_C], w_in[:, g1:g1 + H_D],
                              jnp.zeros((d, LANES - n_g), w_in.dtype)], axis=1).astype(BF16)
    gbias = jnp.concatenate([c_gate_b[0], c_gate_b[1], d_f_b, jnp.zeros((LANES - n_g,), F32)])[None]
    return dict(w_in=w_main, w_gate=w_gate, gbias=gbias, fbias=d_f_b[None, :],
                wa=w_out[:d_c].astype(BF16), wb=w_out[d_c:].astype(BF16), d_c=d_c, d_d=d_d)


def _odd_prompt(x, g_pre, g_post, w, shift=None):
    s = x.shape[0]
    d_c, d_d = w["d_c"], w["d_d"]
    z, gates = _norm_proj(x, g_pre, w["w_in"], w["w_gate"], tm=_row_tile(s, 1024), tn=1024)
    yc, c_new, n_new, m8 = _mlstm_prompt(z, gates, w["gbias"], d_c=d_c)
    logf, decay = _fox_gates(gates, w["gbias"], tb=_row_tile(s, 512))
    yd = _fox_prompt(z, decay, shift, col0=5 * d_c // LANES, tq=_row_tile(s, 1024), tk=_row_tile(s, 512))
    shifted = None
    if shift is not None:
        yd, *shifted = yd
    y = _out_proj(yc, yd, w["wa"], w["wb"], g_post, x, tm=_row_tile(s, 512))
    k_d = z[:, 5 * d_c + d_d:5 * d_c + 2 * d_d]
    v_d = z[:, 5 * d_c + 2 * d_d:5 * d_c + 3 * d_d]
    return y, c_new, n_new, m8[:H_C, 0], k_d, v_d, logf[:, 2 * H_C:2 * H_C + H_D], shifted


def _odd_sample(x, g_pre, g_post, w, c0, n0, m0, k_pool, v_pool, lf_pool, page_table):
    bsz = x.shape[0]
    d_c, d_d = w["d_c"], w["d_d"]
    z, gates = _norm_proj(x, g_pre, w["w_in"], w["w_gate"], tm=bsz, tn=1024)
    z3 = z.reshape(bsz, 1, -1)
    yc, c_new, n_new, m_new, lf_row = _mlstm_sample(z3, gates.reshape(bsz, 1, LANES), w["gbias"],
                                                     c0, n0, m0.reshape(bsz, 1, H_C))
    n_pool, page, n_h, hd = k_pool.shape
    width = page * n_h
    suf, tot = _page_sums(lf_pool.reshape(n_pool, width), rows=math.gcd(n_pool, 256))
    lfn_row = jnp.tile(gates[:, 2 * H_C:2 * H_C + H_D], (1, page)).reshape(bsz, 1, width)
    yd = _fox_sample(page_table, z.reshape(bsz, -1, hd), lfn_row, jnp.tile(w["fbias"], (1, page)),
                     k_pool.reshape(n_pool, width, hd), v_pool.reshape(n_pool, width, hd),
                     suf.reshape(n_pool, 1, width), tot.reshape(n_pool, 1, width),
                     row0=5 * d_c // d_d, pages_per_step=8)
    y = _out_proj(yc.reshape(bsz, d_c), yd.reshape(bsz, d_d), w["wa"], w["wb"], g_post, x, tm=bsz)
    k_d = z[:, 5 * d_c + d_d:5 * d_c + 2 * d_d]
    v_d = z[:, 5 * d_c + 2 * d_d:5 * d_c + 3 * d_d]
    return y, c_new, n_new, m_new.reshape(bsz, H_C), k_d, v_d, lf_row[:, 0, 2 * H_C:2 * H_C + H_D]


def kernel(x_prompt, x_sample, state_a_h, state_a_conv, cache_b_k, cache_b_v, state_c_C, state_c_n,
           state_c_m, cache_d_k, cache_d_v, cache_d_logf, page_table, norm_pre, norm_post, w_in_even,
           w_out_even, a_conv_w, a_conv_b, a_gate_w, a_gate_b, a_lambda, rel_bias, w_in_odd, w_out_odd,
           c_gate_b, d_f_b):
    bp, s, _ = x_prompt.shape
    bs = x_sample.shape[0]
    assert x_sample.shape[1] == 1
    xp = [x_prompt[b] for b in range(bp)]
    xs = x_sample[:, 0]
    names = ("ah", "ac", "bk", "bv", "cC", "cn", "cm", "dk", "dv", "dl")
    outs_p = {n: [] for n in names}
    outs_s = {n: [] for n in names}
    hb = cache_b_k.shape[3:]
    hd = cache_d_k.shape[3:]
    depth = norm_pre.shape[0]
    pending_shift = None
    for l in range(depth):
        j = l // 2
        g_pre, g_post = norm_pre[l][None], norm_post[l][None]
        if l % 2 == 0:
            w = _even_weights(w_in_even[j], w_out_even[j], a_conv_w[j], a_conv_b[j], a_gate_w[j],
                              a_gate_b[j], a_lambda[j], rel_bias)
            res = [_even_prompt(x, g_pre, g_post, w) for x in xp]
            xp = [r[0] for r in res]
            outs_p["ah"].append(jnp.concatenate([r[1] for r in res], axis=0))
            outs_p["ac"].append(jnp.stack([r[2] for r in res]))
            outs_p["bk"].append(jnp.stack([r[3].reshape(-1, *hb) for r in res]))
            outs_p["bv"].append(jnp.stack([r[4].reshape(-1, *hb) for r in res]))
            defer = l + 1 < depth
            xs, h_new, conv_new, k_out, v_out = _even_sample(
                xs, g_pre, g_post, w, state_a_h[j], state_a_conv[j], cache_b_k[j], cache_b_v[j],
                shift_now=not defer)
            outs_s["ah"].append(h_new)
            outs_s["ac"].append(conv_new)
            if defer:
                pending_shift = (cache_b_k[j], cache_b_v[j], k_out, v_out)
            else:
                outs_s["bk"].append(k_out)
                outs_s["bv"].append(v_out)
        else:
            w = _odd_weights(w_in_odd[j], w_out_odd[j], c_gate_b[j], d_f_b[j])
            res = [_odd_prompt(x, g_pre, g_post, w, pending_shift if b == 0 else None)
                   for b, x in enumerate(xp)]
            if pending_shift is not None:
                k_out, v_out = res[0][7]
                outs_s["bk"].append(k_out)
                outs_s["bv"].append(v_out)
                pending_shift = None
            xp = [r[0] for r in res]
            outs_p["cC"].append(jnp.stack([r[1] for r in res]))
            outs_p["cn"].append(jnp.stack([r[2] for r in res]))
            outs_p["cm"].append(jnp.stack([r[3] for r in res]))
            outs_p["dk"].append(jnp.stack([r[4].reshape(s, *hd) for r in res]))
            outs_p["dv"].append(jnp.stack([r[5].reshape(s, *hd) for r in res]))
            outs_p["dl"].append(jnp.stack([r[6] for r in res]))
            xs, c_new, n_new, m_new, k_d, v_d, lf_d = _odd_sample(
                xs, g_pre, g_post, w, state_c_C[j], state_c_n[j], state_c_m[j],
                cache_d_k[j], cache_d_v[j], cache_d_logf[j], page_table)
            outs_s["cC"].append(c_new)
            outs_s["cn"].append(n_new)
            outs_s["cm"].append(m_new)
            outs_s["dk"].append(k_d.reshape(bs, 1, *hd))
            outs_s["dv"].append(v_d.reshape(bs, 1, *hd))
            outs_s["dl"].append(lf_d.reshape(bs, 1, -1))
    st = jnp.stack
    return (st(xp), xs[:, None, :],
            st(outs_p["ah"]), st(outs_s["ah"]), st(outs_p["ac"]), st(outs_s["ac"]),
            st(outs_p["bk"]), st(outs_p["bv"]), st(outs_s["bk"]), st(outs_s["bv"]),
            st(outs_p["cC"]), st(outs_p["cn"]), st(outs_p["cm"]),
            st(outs_s["cC"]), st(outs_s["cn"]), st(outs_s["cm"]),
            st(outs_p["dk"]), st(outs_p["dv"]), st(outs_p["dl"]),
            st(outs_s["dk"]), st(outs_s["dv"]), st(outs_s["dl"]))
```
